```python
import jax, jax.numpy as jnp
from jax import lax
import numpy as np

D_MODEL = 2048
BATCH = 4
SEQ = 2048
DEPTH = 1
DEC_BATCH = 32
DEC_SEQ = 8
PAST_LEN = 16384
PAGE_SIZE = 128

HEAD_DIM = 128
ATT_W = D_MODEL // 2
N_HEADS_A = ATT_W // HEAD_DIM
POOL_W = D_MODEL // 2
POOL_WINDOWS = (2, 4, 8, 16)
POOL_GROUPS = len(POOL_WINDOWS)
POOL_GROUP = POOL_W // POOL_GROUPS
POOL_BUF = max(POOL_WINDOWS) - 1
N_BRANCH = 2
IN_W = 3 * ATT_W + POOL_W + N_BRANCH * D_MODEL
D_FF = ((8 * D_MODEL + 3 * 256 - 1) // (3 * 256)) * 256
Q_BLOCK = 128
EPS = 1e-6
SB_SCALE = HEAD_DIM ** -0.5
SB_BIAS_INIT = -6.0

kernel_name = "stick_breaking_pool_hybrid_step"

F32 = jnp.float32


def _rmsnorm(x, g):
    x32 = x.astype(F32)
    y = x32 * lax.rsqrt(jnp.mean(x32 * x32, axis=-1, keepdims=True) + EPS) * g.astype(F32)
    return y.astype(x.dtype)


def _stick_breaking(q, k_new, v_new, k_past, v_past, sb_bias):
    T = q.shape[1]
    P = k_past.shape[1]
    bias = sb_bias.astype(F32)[None, :, None, None]
    outs = []
    for start in range(0, T, Q_BLOCK):
        end = min(start + Q_BLOCK, T)
        qb = q[:, start:end]
        kn, vn = k_new[:, :end], v_new[:, :end]
        z = jnp.einsum('bqhd,bkhd->bhqk', qb, kn, preferred_element_type=F32)
        if P > 0:
            z_past = jnp.einsum('bqhd,bkhd->bhqk', qb, k_past, preferred_element_type=F32)
            z = jnp.concatenate([z_past, z], axis=-1)
        z = z * SB_SCALE + bias
        q_pos = P + jnp.arange(start, end)
        k_pos = jnp.arange(P + end)
        mask = k_pos[None, :] < q_pos[:, None]
        log_beta = jax.nn.log_sigmoid(z)
        log_stay = jnp.where(mask, jax.nn.log_sigmoid(-z), 0.0)
        log_after = lax.cumsum(log_stay, axis=3, reverse=True) - log_stay
        w = jnp.where(mask, jnp.exp(log_beta + log_after), 0.0).astype(v_new.dtype)
        o = jnp.einsum('bhqk,bkhd->bqhd', w[..., P:], vn, preferred_element_type=F32)
        if P > 0:
            o = o + jnp.einsum('bhqk,bkhd->bqhd', w[..., :P], v_past, preferred_element_type=F32)
        outs.append(o.astype(q.dtype))
    return jnp.concatenate(outs, axis=1)


def _pool_mix(u, buf, past_len, w_pool_group, pool_scale):
    B, T, _ = u.shape
    u_ext = jnp.concatenate([buf, u], axis=1)
    u32 = u_ext.astype(F32)
    cs = lax.cumsum(jnp.pad(u32, ((0, 0), (1, 0), (0, 0))), axis=1)
    pos = past_len + jnp.arange(T)
    outs = []
    for g, win in enumerate(POOL_WINDOWS):
        lo, hi = g * POOL_GROUP, (g + 1) * POOL_GROUP
        wsum = (cs[:, POOL_BUF + 1:POOL_BUF + 1 + T, lo:hi]
                - cs[:, POOL_BUF + 1 - win:POOL_BUF + 1 - win + T, lo:hi])
        cnt = jnp.minimum(win, pos + 1).astype(F32)[None, :, None]
        outs.append(wsum / cnt - u32[:, POOL_BUF:, lo:hi])
    pooled = jnp.stack(outs, axis=2).astype(u.dtype)
    mixed = jnp.einsum('btgc,gcd->btgd', pooled, w_pool_group).reshape(B, T, POOL_W) * pool_scale
    return mixed, u_ext[:, -POOL_BUF:]


def _layer(x, c, k_past, v_past, pool_buf, past_len,
           w_cond, b_cond, g_mix_pre, g_mix_post, g_ffn_pre, g_ffn_post,
           w_in, b_gate, sb_bias, w_pool_group, pool_scale, w_branch_att, w_branch_pool,
           w_out, w_ffn_in, w_ffn_out):
    B, T, _ = x.shape
    mod = (jax.nn.silu(c) @ w_cond + b_cond)[:, None, :]
    sh1, sc1, gt1, sh2, sc2, gt2 = jnp.split(mod, 6, axis=-1)

    h = _rmsnorm(x, g_mix_pre) * (1.0 + sc1) + sh1
    proj = h @ w_in
    q, k, v, u, gl = jnp.split(proj, [ATT_W, 2 * ATT_W, 3 * ATT_W, 3 * ATT_W + POOL_W], axis=-1)
    q = q.reshape(B, T, N_HEADS_A, HEAD_DIM)
    k = k.reshape(B, T, N_HEADS_A, HEAD_DIM)
    v = v.reshape(B, T, N_HEADS_A, HEAD_DIM)
    a = _stick_breaking(q, k, v, k_past, v_past, sb_bias).reshape(B, T, ATT_W)
    p, new_buf = _pool_mix(u, pool_buf, past_len, w_pool_group, pool_scale)
    gates = jax.nn.sigmoid((gl + b_gate).astype(F32)).astype(x.dtype)
    g_a, g_p = jnp.split(gates, N_BRANCH, axis=-1)
    merged = g_a * (a @ w_branch_att) + g_p * (p @ w_branch_pool)
    x = x + gt1 * _rmsnorm(merged @ w_out, g_mix_post)

    h2 = _rmsnorm(x, g_ffn_pre) * (1.0 + sc2) + sh2
    gg, uu = jnp.split(h2 @ w_ffn_in, 2, axis=-1)
    ff = (jax.nn.silu(gg) * uu) @ w_ffn_out
    x = x + gt2 * _rmsnorm(ff, g_ffn_post)
    return x, k, v, new_buf


def setup_inputs(seed: int = 0) -> dict:
    key = jax.random.key(seed)
    ks = iter(jax.random.split(key, 32))
    n_pages = PAST_LEN // PAGE_SIZE
    n_used = DEC_BATCH * n_pages
    n_phys = n_used + (n_used + 3) // 4
    nrm = lambda shape, s=1.0: jax.random.normal(next(ks), shape, F32) * s
    D = D_MODEL
    x_prompt = nrm((BATCH, SEQ, D))
    x_sample = nrm((DEC_BATCH, DEC_SEQ, D))
    c_prompt = nrm((BATCH, D))
    c_sample = nrm((DEC_BATCH, D))
    cache_k = nrm((DEPTH, n_phys, PAGE_SIZE, N_HEADS_A, HEAD_DIM))
    cache_v = nrm((DEPTH, n_phys, PAGE_SIZE, N_HEADS_A, HEAD_DIM))
    page_table = jax.random.permutation(next(ks), n_phys)[:n_used].reshape(DEC_BATCH, n_pages).astype(jnp.int32)
    state_pool = nrm((DEPTH, DEC_BATCH, POOL_BUF, POOL_W))
    return {
        "x_prompt": x_prompt, "x_sample": x_sample,
        "c_prompt": c_prompt, "c_sample": c_sample,
        "cache_k": cache_k, "cache_v": cache_v, "page_table": page_table,
        "state_pool": state_pool,
        "w_cond": nrm((DEPTH, D, 6 * D), 0.5 * D ** -0.5),
        "b_cond": nrm((DEPTH, 6 * D), 0.02),
        "g_mix_pre": 1.0 + nrm((DEPTH, D), 0.05),
        "g_mix_post": 1.0 + nrm((DEPTH, D), 0.05),
        "g_ffn_pre": 1.0 + nrm((DEPTH, D), 0.05),
        "g_ffn_post": 1.0 + nrm((DEPTH, D), 0.05),
        "w_in": nrm((DEPTH, D, IN_W), D ** -0.5),
        "b_gate": nrm((DEPTH, N_BRANCH * D), 0.02),
        "sb_bias": SB_BIAS_INIT + nrm((DEPTH, N_HEADS_A), 0.1),
        "w_pool_group": nrm((DEPTH, POOL_GROUPS, POOL_GROUP, POOL_GROUP), POOL_GROUP ** -0.5),
        "pool_scale": 1.0 + nrm((DEPTH, POOL_W), 0.05),
        "w_branch_att": nrm((DEPTH, ATT_W, D), ATT_W ** -0.5),
        "w_branch_pool": nrm((DEPTH, POOL_W, D), POOL_W ** -0.5),
        "w_out": nrm((DEPTH, D, D), D ** -0.5),
        "w_ffn_in": nrm((DEPTH, D, 2 * D_FF), D ** -0.5),
        "w_ffn_out": nrm((DEPTH, D_FF, D), D_FF ** -0.5),
    }


def reference(x_prompt, x_sample, c_prompt, c_sample, cache_k, cache_v, page_table, state_pool,
              w_cond, b_cond, g_mix_pre, g_mix_post, g_ffn_pre, g_ffn_post,
              w_in, b_gate, sb_bias, w_pool_group, pool_scale, w_branch_att, w_branch_pool,
              w_out, w_ffn_in, w_ffn_out):
    b_p = x_prompt.shape[0]
    b_s, n_pages = page_table.shape
    past_len = n_pages * cache_k.shape[2]
    yp, ys = x_prompt, x_sample
    kp_l, vp_l, bp_l, ks_l, vs_l, bs_l = [], [], [], [], [], []
    for l in range(DEPTH):
        lw = (w_cond[l], b_cond[l], g_mix_pre[l], g_mix_post[l], g_ffn_pre[l], g_ffn_post[l],
              w_in[l], b_gate[l], sb_bias[l], w_pool_group[l], pool_scale[l], w_branch_att[l],
              w_branch_pool[l], w_out[l], w_ffn_in[l], w_ffn_out[l])
        empty_kv = jnp.zeros((b_p, 0, N_HEADS_A, HEAD_DIM), x_prompt.dtype)
        zero_buf = jnp.zeros((b_p, POOL_BUF, POOL_W), x_prompt.dtype)
        yp, kp, vp, bp = _layer(yp, c_prompt, empty_kv, empty_kv, zero_buf, 0, *lw)
        k_past = cache_k[l][page_table].reshape(b_s, past_len, N_HEADS_A, HEAD_DIM)
        v_past = cache_v[l][page_table].reshape(b_s, past_len, N_HEADS_A, HEAD_DIM)
        ys, ksn, vsn, bsn = _layer(ys, c_sample, k_past, v_past, state_pool[l], past_len, *lw)
        kp_l.append(kp); vp_l.append(vp); bp_l.append(bp)
        ks_l.append(ksn); vs_l.append(vsn); bs_l.append(bsn)
    new_k_prompt = jnp.stack(kp_l)
    new_v_prompt = jnp.stack(vp_l)
    new_pool_prompt = jnp.stack(bp_l)
    new_k_sample = jnp.stack(ks_l)
    new_v_sample = jnp.stack(vs_l)
    new_pool_sample = jnp.stack(bs_l)
    return (yp, ys, new_k_prompt, new_v_prompt, new_pool_prompt, new_k_sample, new_v_sample, new_pool_sample)
```

```python
import functools

import jax
import jax.numpy as jnp
from jax import lax
from jax.experimental import pallas as pl
from jax.experimental.pallas import tpu as pltpu

F32 = jnp.float32
BF16 = jnp.bfloat16

HEAD_DIM = 128
POOL_WINDOWS = (2, 4, 8, 16)
POOL_BUF = max(POOL_WINDOWS) - 1
POOL_HIST = POOL_BUF + 1
EPS = 1e-6
SB_SCALE = HEAD_DIM ** -0.5

VMEM_LIMIT_BYTES = 60 * 1024 * 1024
ATT_BLOCK = 256
PAGES_PER_STEP = 8
TILE_INPROJ = 1024
TILE_POOL = 512
TILE_MERGE = 1024
TILE_OUTPROJ = 512
TILE_FFN = 1024


def _cparams(semantics):
    return pltpu.CompilerParams(dimension_semantics=semantics, vmem_limit_bytes=VMEM_LIMIT_BYTES)


def _dot(a, b):
    return jnp.dot(a, b, preferred_element_type=F32)


def _dot_nt(a, b):
    return lax.dot_general(a, b, (((1,), (1,)), ((), ())), preferred_element_type=F32)


def _rms(x, g):
    return x * lax.rsqrt(jnp.mean(x * x, axis=-1, keepdims=True) + EPS) * g


def _mod_kernel(c_ref, w_ref, b_ref, o_ref):
    o_ref[...] = _dot(jax.nn.silu(c_ref[...]), w_ref[...]) + b_ref[...]


def _modulation(c_all, w_cond, b_cond, tn=1024):
    m, d = c_all.shape
    n = w_cond.shape[1]
    return pl.pallas_call(
        _mod_kernel,
        grid=(n // tn,),
        in_specs=[pl.BlockSpec((m, d), lambda j: (0, 0)),
                  pl.BlockSpec((d, tn), lambda j: (0, j)),
                  pl.BlockSpec((1, tn), lambda j: (0, j))],
        out_specs=pl.BlockSpec((m, tn), lambda j: (0, j)),
        out_shape=jax.ShapeDtypeStruct((m, n), F32),
        compiler_params=_cparams(("arbitrary",)),
        name="modulation",
    )(c_all, w_cond, b_cond.reshape(1, n))


class _Rows:
    def __init__(self, n_seq, seq_len, tile, mod_row0):
        if seq_len >= tile:
            assert seq_len % tile == 0
            self.G, self.R = 1, tile
            self.tiles_per_seq = seq_len // tile
            self.n_tiles = n_seq * self.tiles_per_seq
            self.mod_map = lambda i, k: (mod_row0 + i // self.tiles_per_seq, 0, k)
        else:
            assert seq_len % 8 == 0 and n_seq * seq_len <= tile
            self.G, self.R = n_seq, seq_len
            self.tiles_per_seq = 1
            self.n_tiles = 1
            mod_blk = mod_row0 // n_seq
            assert mod_blk * n_seq == mod_row0
            self.mod_map = lambda i, k: (mod_blk, 0, k)
        self.tm = self.G * self.R
        self.n_rows = n_seq * seq_len

    def x_spec(self, d, n_grid=2, **kw):
        if n_grid == 2:
            return pl.BlockSpec((self.G, self.R, d), lambda i, j: (i, 0, 0), **kw)
        return pl.BlockSpec((self.G, self.R, d), lambda i: (i, 0, 0), **kw)

    def mod_spec(self, d, k, n_grid=2):
        if n_grid == 2:
            return pl.BlockSpec((self.G, 1, d), lambda i, j: self.mod_map(i, k))
        return pl.BlockSpec((self.G, 1, d), lambda i: self.mod_map(i, k))

    def view(self, x):
        return x.reshape(self.n_tiles * self.G, self.R, x.shape[-1])


def _inproj_kernel(x_ref, sh_ref, sc_ref, g_ref, w_ref, o_ref, h_ref):
    @pl.when(pl.program_id(1) == 0)
    def _():
        h = _rms(x_ref[...], g_ref[...]) * (1.0 + sc_ref[...]) + sh_ref[...]
        h_ref[...] = h.reshape(h_ref.shape).astype(h_ref.dtype)

    o_ref[...] = _dot(h_ref[...], w_ref[...])


def _inproj(rows, x3, mod3, g_pre, w_in, tn=512):
    d = x3.shape[-1]
    n_out = w_in.shape[1]
    return pl.pallas_call(
        _inproj_kernel,
        grid=(rows.n_tiles, n_out // tn),
        in_specs=[rows.x_spec(d, pipeline_mode=pl.Buffered(1)),
                  rows.mod_spec(d, 0), rows.mod_spec(d, 1),
                  pl.BlockSpec((1, d), lambda i, j: (0, 0)),
                  pl.BlockSpec((d, tn), lambda i, j: (0, j))],
        out_specs=pl.BlockSpec((rows.tm, tn), lambda i, j: (i, j)),
        out_shape=jax.ShapeDtypeStruct((rows.n_rows, n_out), F32),
        scratch_shapes=[pltpu.VMEM((rows.tm, d), BF16)],
        compiler_params=_cparams(("parallel", "arbitrary")),
        name="inproj",
    )(x3, mod3, mod3, g_pre.reshape(1, d), w_in)


def _sb_block(z, mask, tri, carry):
    t = jnp.log1p(jnp.exp(-jnp.abs(z)))
    log_beta = jnp.minimum(z, 0.0) - t
    log_stay = -jnp.maximum(z, 0.0) - t
    if mask is not None:
        log_stay = jnp.where(mask, log_stay, 0.0)
    hi = log_stay.astype(BF16)
    lo = (log_stay - hi.astype(F32)).astype(BF16)
    after = _dot(hi, tri) + _dot(lo, tri) + carry
    w = jnp.exp(log_beta + after)
    if mask is not None:
        w = jnp.where(mask, w, 0.0)
    return w, carry + jnp.sum(log_stay, axis=-1, keepdims=True)


def _tri(k):
    r = lax.broadcasted_iota(jnp.int32, (k, k), 0)
    c = lax.broadcasted_iota(jnp.int32, (k, k), 1)
    return (r > c).astype(BF16)


def _attn_prompt_kernel(bias_ref, tri_ref, q_ref, k_ref, v_ref, o_ref):
    blk = q_ref.shape[0]
    h = pl.program_id(1)
    qi = pl.program_id(2)
    bias = bias_ref[h]
    q = q_ref[...] * SB_SCALE
    tri = tri_ref[...]

    def visit(start, mask, carry):
        c, acc = carry
        k = k_ref[pl.ds(start, blk), :]
        v = v_ref[pl.ds(start, blk), :]
        w, c = _sb_block(_dot_nt(q, k) + bias, mask, tri, c)
        return c, acc + _dot(w, v)

    r = lax.broadcasted_iota(jnp.int32, (blk, blk), 0)
    s = lax.broadcasted_iota(jnp.int32, (blk, blk), 1)
    carry = (jnp.zeros((blk, 1), F32), jnp.zeros((blk, HEAD_DIM), F32))
    carry = visit(pl.multiple_of(qi * blk, blk), s < r, carry)

    def body(it, carry):
        return visit(pl.multiple_of((qi - 1 - it) * blk, blk), None, carry)

    _, acc = lax.fori_loop(0, qi, body, carry)
    o_ref[...] = acc


def _attn_prompt(proj, sb_bias, n_seq, seq_len, n_heads, blk=ATT_BLOCK):
    nq = seq_len // blk
    att_w = n_heads * HEAD_DIM
    return pl.pallas_call(
        _attn_prompt_kernel,
        grid=(n_seq, n_heads, nq),
        in_specs=[pl.BlockSpec(memory_space=pltpu.SMEM),
                  pl.BlockSpec((blk, blk), lambda b, h, i: (0, 0)),
                  pl.BlockSpec((blk, HEAD_DIM), lambda b, h, i: (b * nq + i, h)),
                  pl.BlockSpec((seq_len, HEAD_DIM), lambda b, h, i: (b, n_heads + h)),
                  pl.BlockSpec((seq_len, HEAD_DIM), lambda b, h, i: (b, 2 * n_heads + h))],
        out_specs=pl.BlockSpec((blk, HEAD_DIM), lambda b, h, i: (b * nq + i, h)),
        out_shape=jax.ShapeDtypeStruct((n_seq * seq_len, att_w), F32),
        compiler_params=_cparams(("parallel", "parallel", "arbitrary")),
        name="attn_prompt",
    )(sb_bias, _tri(blk), proj, proj, proj)


def _attn_sample_kernel(pt_ref, bias_ref, tri_ref, q_ref, kn_ref, vn_ref, *rest, n_heads, t_new, pages):
    k_refs, v_refs = rest[:pages], rest[pages:2 * pages]
    o_ref, qs_ref, c_ref, acc_ref = rest[2 * pages:]
    s = pl.program_id(1)
    page = k_refs[0].shape[0] // n_heads
    m = n_heads * t_new
    tri = tri_ref[...]

    def head_rows(ref, h):
        return ref[pl.ds(h, page, stride=n_heads), :]

    def visit(k_of, v_of, mask):
        z = jnp.concatenate(
            [_dot_nt(qs_ref[h], k_of(h)) + bias_ref[h] for h in range(n_heads)], axis=0)
        w, c = _sb_block(z, mask, tri, c_ref[...])
        c_ref[...] = c
        for h in range(n_heads):
            rows = slice(h * t_new, (h + 1) * t_new)
            acc_ref[rows, :] += _dot(w[rows, :], v_of(h))

    @pl.when(s == 0)
    def _():
        for h in range(n_heads):
            qs_ref[h] = q_ref[:, h * HEAD_DIM:(h + 1) * HEAD_DIM] * SB_SCALE
        c_ref[...] = jnp.zeros_like(c_ref)
        acc_ref[...] = jnp.zeros_like(acc_ref)
        pad = jnp.zeros((page - t_new, HEAD_DIM), F32)
        q_idx = lax.broadcasted_iota(jnp.int32, (m, page), 0) % t_new
        k_idx = lax.broadcasted_iota(jnp.int32, (m, page), 1)
        visit(lambda h: jnp.concatenate([kn_ref[:, h * HEAD_DIM:(h + 1) * HEAD_DIM], pad], axis=0),
              lambda h: jnp.concatenate([vn_ref[:, h * HEAD_DIM:(h + 1) * HEAD_DIM], pad], axis=0),
              k_idx < q_idx)

    for i in range(pages):
        visit(functools.partial(head_rows, k_refs[i]), functools.partial(head_rows, v_refs[i]), None)

    @pl.when(s == pl.num_programs(1) - 1)
    def _():
        for h in range(n_heads):
            o_ref[:, h * HEAD_DIM:(h + 1) * HEAD_DIM] = acc_ref[h * t_new:(h + 1) * t_new, :]


def _attn_sample(proj, cache_k, cache_v, page_table, sb_bias, n_seq, t_new, n_heads, pages=PAGES_PER_STEP):
    n_phys, page = cache_k.shape[0], cache_k.shape[1]
    n_pages = page_table.shape[1]
    assert n_pages % pages == 0
    att_w = n_heads * HEAD_DIM
    rows = page * n_heads
    ck = cache_k.reshape(n_phys, rows, HEAD_DIM)
    cv = cache_v.reshape(n_phys, rows, HEAD_DIM)

    def page_spec(i):
        return pl.BlockSpec((None, rows, HEAD_DIM),
                            lambda b, s, pt: (pt[b, n_pages - 1 - (s * pages + i)], 0, 0))

    grid_spec = pltpu.PrefetchScalarGridSpec(
        num_scalar_prefetch=1,
        grid=(n_seq, n_pages // pages),
        in_specs=[pl.BlockSpec(memory_space=pltpu.SMEM),
                  pl.BlockSpec((page, page), lambda b, s, pt: (0, 0)),
                  pl.BlockSpec((t_new, att_w), lambda b, s, pt: (b, 0)),
                  pl.BlockSpec((t_new, att_w), lambda b, s, pt: (b, 1)),
                  pl.BlockSpec((t_new, att_w), lambda b, s, pt: (b, 2))]
                 + [page_spec(i) for i in range(pages)] * 2,
        out_specs=pl.BlockSpec((t_new, att_w), lambda b, s, pt: (b, 0)),
        scratch_shapes=[pltpu.VMEM((n_heads, t_new, HEAD_DIM), F32),
                        pltpu.VMEM((n_heads * t_new, 1), F32),
                        pltpu.VMEM((n_heads * t_new, HEAD_DIM), F32)],
    )
    kern = functools.partial(_attn_sample_kernel, n_heads=n_heads, t_new=t_new, pages=pages)
    return pl.pallas_call(
        kern,
        grid_spec=grid_spec,
        out_shape=jax.ShapeDtypeStruct((n_seq * t_new, att_w), F32),
        compiler_params=_cparams(("parallel", "arbitrary")),
        name="attn_sample",
    )(page_table, sb_bias, _tri(page), proj, proj, proj, *([ck] * pages), *([cv] * pages))


def _pool_kernel(u_ref, hist_ref, wg_ref, ps_ref, o_ref, ext_ref, pa_ref, pb_ref, *, G, R, tiles_per_seq,
                 past_len, fresh):
    i = pl.program_id(0)
    c_all = u_ref.shape[-1]
    cg = c_all // len(POOL_WINDOWS)
    n = POOL_HIST + R
    cur = u_ref[...].reshape(G, R, c_all)
    hist = hist_ref[...]
    t0 = 0
    if fresh:
        t0 = (i % tiles_per_seq) * R
        hist = jnp.where(t0 == 0, 0.0, hist)
    ext_ref[:, 0:POOL_HIST, :] = hist
    ext_ref[:, POOL_HIST:n, :] = cur

    src, dst = ext_ref, pa_ref
    for lvl, win in enumerate(POOL_WINDOWS):
        half = win // 2
        lo = lvl * cg
        rows_out = n - (win - 1)
        dst[:, 0:rows_out, lo:] = src[:, half:half + rows_out, lo:] + src[:, 0:rows_out, lo:]
        src, dst = dst, (pb_ref if dst is pa_ref else pa_ref)

    pos = past_len + t0 + lax.broadcasted_iota(jnp.int32, (1, R, cg), 1)
    for g, win in enumerate(POOL_WINDOWS):
        buf = pa_ref if g % 2 == 0 else pb_ref
        cols = slice(g * cg, (g + 1) * cg)
        first = POOL_HIST - (win - 1)
        wsum = buf[:, first:first + R, cols]
        cnt = jnp.minimum(win, pos + 1).astype(F32)
        pooled = wsum / cnt - cur[:, :, cols]
        mixed = _dot(pooled.reshape(G * R, cg), wg_ref[g])
        o_ref[:, cols] = mixed * ps_ref[:, cols]


def _pool(rows, proj, hist3, hist_map, w_pool_group, pool_scale, pool_w, u_col_block, past_len, fresh):
    n_groups, cg = w_pool_group.shape[0], w_pool_group.shape[1]
    G, R = rows.G, rows.R
    kern = functools.partial(_pool_kernel, G=G, R=R, tiles_per_seq=rows.tiles_per_seq,
                             past_len=past_len, fresh=fresh)
    ext = pltpu.VMEM((G, POOL_HIST + R, pool_w), F32)
    return pl.pallas_call(
        kern,
        grid=(rows.n_tiles,),
        in_specs=[pl.BlockSpec((rows.tm, pool_w), lambda i: (i, u_col_block)),
                  pl.BlockSpec((G, POOL_HIST, pool_w), hist_map),
                  pl.BlockSpec((n_groups, cg, cg), lambda i: (0, 0, 0)),
                  pl.BlockSpec((1, pool_w), lambda i: (0, 0))],
        out_specs=pl.BlockSpec((rows.tm, pool_w), lambda i: (i, 0)),
        out_shape=jax.ShapeDtypeStruct((rows.n_rows, pool_w), F32),
        scratch_shapes=[ext, ext, ext],
        compiler_params=_cparams(("parallel",)),
        name="pool",
    )(proj, hist3, w_pool_group, pool_scale.reshape(1, pool_w))


def _merge_kernel(a_ref, p_ref, gla_ref, glp_ref, bga_ref, bgp_ref, wa_ref, wp_ref, o_ref):
    ga = jax.nn.sigmoid(gla_ref[...] + bga_ref[...])
    gp = jax.nn.sigmoid(glp_ref[...] + bgp_ref[...])
    o_ref[...] = ga * _dot(a_ref[...], wa_ref[...]) + gp * _dot(p_ref[...], wp_ref[...])


def _merge(rows, a, p, proj, b_gate, w_a, w_p, gl_col0, tn=512):
    d = w_a.shape[1]
    tm = rows.tm
    gl_blk = gl_col0 // tn
    nd = d // tn
    return pl.pallas_call(
        _merge_kernel,
        grid=(rows.n_tiles, nd),
        in_specs=[pl.BlockSpec((tm, a.shape[1]), lambda i, j: (i, 0)),
                  pl.BlockSpec((tm, p.shape[1]), lambda i, j: (i, 0)),
                  pl.BlockSpec((tm, tn), lambda i, j: (i, gl_blk + j)),
                  pl.BlockSpec((tm, tn), lambda i, j: (i, gl_blk + nd + j)),
                  pl.BlockSpec((1, tn), lambda i, j: (0, j)),
                  pl.BlockSpec((1, tn), lambda i, j: (0, nd + j)),
                  pl.BlockSpec((w_a.shape[0], tn), lambda i, j: (0, j)),
                  pl.BlockSpec((w_p.shape[0], tn), lambda i, j: (0, j))],
        out_specs=pl.BlockSpec((tm, tn), lambda i, j: (i, j)),
        out_shape=jax.ShapeDtypeStruct((rows.n_rows, d), F32),
        compiler_params=_cparams(("parallel", "arbitrary")),
        name="merge",
    )(a, p, proj, proj, b_gate.reshape(1, 2 * d), b_gate.reshape(1, 2 * d), w_a, w_p)


def _outproj_kernel(m_ref, w_ref, g_ref, x_ref, gt_ref, o_ref):
    y = _dot(m_ref[...], w_ref[...])
    o_ref[...] = x_ref[...] + gt_ref[...] * _rms(y, g_ref[...]).reshape(x_ref.shape)


def _outproj(rows, merged, w_out, g_post, x3, mod3):
    d = w_out.shape[1]
    return pl.pallas_call(
        _outproj_kernel,
        grid=(rows.n_tiles,),
        in_specs=[pl.BlockSpec((rows.tm, d), lambda i: (i, 0)),
                  pl.BlockSpec((d, d), lambda i: (0, 0), pipeline_mode=pl.Buffered(1)),
                  pl.BlockSpec((1, d), lambda i: (0, 0)),
                  rows.x_spec(d, n_grid=1),
                  rows.mod_spec(d, 2, n_grid=1)],
        out_specs=rows.x_spec(d, n_grid=1),
        out_shape=jax.ShapeDtypeStruct(x3.shape, F32),
        compiler_params=_cparams(("parallel",)),
        name="outproj",
    )(merged, w_out, g_post.reshape(1, d), x3, mod3)


def _ffn_kernel(x_ref, sh_ref, sc_ref, gt_ref, gpre_ref, gpost_ref, wg_ref, wu_ref, wo_ref, o_ref, h_ref, acc_ref):
    j = pl.program_id(1)

    @pl.when(j == 0)
    def _():
        h = _rms(x_ref[...], gpre_ref[...]) * (1.0 + sc_ref[...]) + sh_ref[...]
        h_ref[...] = h.reshape(h_ref.shape).astype(h_ref.dtype)
        acc_ref[...] = jnp.zeros_like(acc_ref)

    h = h_ref[...]
    act = jax.nn.silu(_dot(h, wg_ref[...])) * _dot(h, wu_ref[...])
    acc_ref[...] += _dot(act, wo_ref[...])

    @pl.when(j == pl.num_programs(1) - 1)
    def _():
        o_ref[...] = x_ref[...] + gt_ref[...] * _rms(acc_ref[...], gpost_ref[...]).reshape(x_ref.shape)


def _ffn(rows, x3, mod3, g_pre, g_post, w_ffn_in, w_ffn_out, tf=256):
    d = x3.shape[-1]
    d_ff = w_ffn_out.shape[0]
    nf = d_ff // tf
    assert nf * tf == d_ff
    return pl.pallas_call(
        _ffn_kernel,
        grid=(rows.n_tiles, nf),
        in_specs=[rows.x_spec(d, pipeline_mode=pl.Buffered(1)),
                  rows.mod_spec(d, 3), rows.mod_spec(d, 4), rows.mod_spec(d, 5),
                  pl.BlockSpec((1, d), lambda i, j: (0, 0)),
                  pl.BlockSpec((1, d), lambda i, j: (0, 0)),
                  pl.BlockSpec((d, tf), lambda i, j: (0, j)),
                  pl.BlockSpec((d, tf), lambda i, j: (0, nf + j)),
                  pl.BlockSpec((tf, d), lambda i, j: (j, 0))],
        out_specs=rows.x_spec(d),
        out_shape=jax.ShapeDtypeStruct(x3.shape, F32),
        scratch_shapes=[pltpu.VMEM((rows.tm, d), BF16), pltpu.VMEM((rows.tm, d), F32)],
        compiler_params=_cparams(("parallel", "arbitrary")),
        name="ffn",
    )(x3, mod3, mod3, mod3, g_pre.reshape(1, d), g_post.reshape(1, d), w_ffn_in, w_ffn_in, w_ffn_out)


def _group_layer(mk_rows, x, mod3, attn_fn, hist_fn, past_len, fresh, lw, n_heads):
    (g_mix_pre, g_mix_post, g_ffn_pre, g_ffn_post, w_in, b_gate, w_pool_group, pool_scale,
     w_branch_att, w_branch_pool, w_out, w_ffn_in, w_ffn_out) = lw
    n_seq, seq_len, d = x.shape
    att_w = n_heads * HEAD_DIM
    pool_w = pool_scale.shape[0]
    rows = mk_rows(TILE_INPROJ)
    proj = _inproj(rows, rows.view(x), mod3, g_mix_pre, w_in)
    a = attn_fn(proj)
    rows = mk_rows(TILE_POOL)
    hist3, hist_map = hist_fn(rows, proj)
    p = _pool(rows, proj, hist3, hist_map, w_pool_group, pool_scale, pool_w,
              u_col_block=3 * att_w // pool_w, past_len=past_len, fresh=fresh)
    merged = _merge(mk_rows(TILE_MERGE), a, p, proj, b_gate, w_branch_att, w_branch_pool,
                    gl_col0=3 * att_w + pool_w)
    rows = mk_rows(TILE_OUTPROJ)
    x1 = _outproj(rows, merged, w_out, g_mix_post, rows.view(x), mod3)
    rows = mk_rows(TILE_FFN)
    x2 = _ffn(rows, rows.view(x1), mod3, g_ffn_pre, g_ffn_post, w_ffn_in, w_ffn_out)
    k = proj[:, att_w:2 * att_w].reshape(n_seq, seq_len, n_heads, HEAD_DIM)
    v = proj[:, 2 * att_w:3 * att_w].reshape(n_seq, seq_len, n_heads, HEAD_DIM)
    u = proj[:, 3 * att_w:3 * att_w + pool_w].reshape(n_seq, seq_len, pool_w)
    return x2.reshape(n_seq, seq_len, d), k, v, u


def kernel(x_prompt, x_sample, c_prompt, c_sample, cache_k, cache_v, page_table, state_pool, w_cond, b_cond, g_mix_pre, g_mix_post, g_ffn_pre, g_ffn_post, w_in, b_gate, sb_bias, w_pool_group, pool_scale, w_branch_att, w_branch_pool, w_out, w_ffn_in, w_ffn_out):
    depth = w_cond.shape[0]
    b_p, t_p, d = x_prompt.shape
    b_s, t_s, _ = x_sample.shape
    n_heads = sb_bias.shape[1]
    pool_w = pool_scale.shape[1]
    past_len = page_table.shape[1] * cache_k.shape[2]
    u_blk = 3 * n_heads * HEAD_DIM // pool_w
    pad = (-(b_s + b_p)) % 8
    c_all = jnp.concatenate([c_sample, c_prompt, jnp.zeros((pad, d), F32)], axis=0)

    yp, ys = x_prompt, x_sample
    outs = [[] for _ in range(6)]
    for l in range(depth):
        lw = (g_mix_pre[l], g_mix_post[l], g_ffn_pre[l], g_ffn_post[l], w_in[l], b_gate[l], w_pool_group[l],
              pool_scale[l], w_branch_att[l], w_branch_pool[l], w_out[l], w_ffn_in[l], w_ffn_out[l])
        mod = _modulation(c_all, w_cond[l], b_cond[l])
        mod3 = mod.reshape(mod.shape[0], 1, mod.shape[1])

        def hist_prompt(rows, proj):
            per_tile = rows.R // POOL_HIST
            hist3 = proj.reshape(proj.shape[0] // POOL_HIST, POOL_HIST, proj.shape[1])
            return hist3, lambda i: (jnp.maximum(i * per_tile - 1, 0), 0, u_blk)

        yp, kp, vp, up = _group_layer(
            lambda tile: _Rows(b_p, t_p, tile, mod_row0=b_s), yp, mod3,
            functools.partial(_attn_prompt, sb_bias=sb_bias[l], n_seq=b_p, seq_len=t_p, n_heads=n_heads),
            hist_prompt, past_len=0, fresh=True, lw=lw, n_heads=n_heads)
        bp = up[:, t_p - POOL_BUF:, :]

        hist_s = jnp.pad(state_pool[l], ((0, 0), (POOL_HIST - POOL_BUF, 0), (0, 0)))
        ys, ksn, vsn, us = _group_layer(
            lambda tile: _Rows(b_s, t_s, tile, mod_row0=0), ys, mod3,
            functools.partial(_attn_sample, cache_k=cache_k[l], cache_v=cache_v[l], page_table=page_table,
                              sb_bias=sb_bias[l], n_seq=b_s, t_new=t_s, n_heads=n_heads),
            lambda rows, proj: (hist_s, lambda i: (0, 0, 0)),
            past_len=past_len, fresh=False, lw=lw, n_heads=n_heads)
        bsn = jnp.concatenate([state_pool[l], us], axis=1)[:, -POOL_BUF:, :]

        for lst, val in zip(outs, (kp, vp, bp, ksn, vsn, bsn)):
            lst.append(val)
    return (yp, ys) + tuple(jnp.stack(o) for o in outs)
```

```python
import functools

import jax
import jax.numpy as jnp
from jax import lax
from jax.experimental import pallas as pl
from jax.experimental.pallas import tpu as pltpu

F32 = jnp.float32
BF16 = jnp.bfloat16

HEAD_DIM = 128
LANES = 128
POOL_WINDOWS = (2, 4, 8, 16)
POOL_BUF = max(POOL_WINDOWS) - 1
POOL_HIST = POOL_BUF + 1
EPS = 1e-6
SB_SCALE = HEAD_DIM ** -0.5

VMEM_LIMIT_BYTES = 60 * 1024 * 1024
ATT_BLOCK = 256
PAGES_PER_STEP = 8
TILE_INPROJ = 1024
TILE_POOL = 512
TILE_MERGE = 1024
TILE_OUTPROJ = 512
TILE_FFN = 1024
COLS_LONG = (512, 512, 256)
COLS_SHORT = (1024, 1024, 512)


def _cparams(semantics):
    return pltpu.CompilerParams(dimension_semantics=semantics, vmem_limit_bytes=VMEM_LIMIT_BYTES)


def _dot(a, b):
    return jnp.dot(a, b, preferred_element_type=F32)


def _dot_nt(a, b):
    return lax.dot_general(a, b, (((1,), (1,)), ((), ())), preferred_element_type=F32)


def _rms(x, g):
    return x * lax.rsqrt(jnp.mean(x * x, axis=-1, keepdims=True) + EPS) * g


def _mod_kernel(c_ref, w_ref, b_ref, o_ref):
    o_ref[...] = _dot(jax.nn.silu(c_ref[...]), w_ref[...]) + b_ref[...]


def _modulation(c_all, w_cond, b_cond, tn=1024):
    m, d = c_all.shape
    n = w_cond.shape[1]
    return pl.pallas_call(
        _mod_kernel,
        grid=(n // tn,),
        in_specs=[pl.BlockSpec((m, d), lambda j: (0, 0)),
                  pl.BlockSpec((d, tn), lambda j: (0, j)),
                  pl.BlockSpec((1, tn), lambda j: (0, j))],
        out_specs=pl.BlockSpec((m, tn), lambda j: (0, j)),
        out_shape=jax.ShapeDtypeStruct((m, n), F32),
        compiler_params=_cparams(("arbitrary",)),
        name="modulation",
    )(c_all, w_cond, b_cond.reshape(1, n))


class _Rows:
    def __init__(self, n_seq, seq_len, tile, mod_row0):
        if seq_len >= tile:
            assert seq_len % tile == 0
            self.G, self.R = 1, tile
            self.tiles_per_seq = seq_len // tile
            self.n_tiles = n_seq * self.tiles_per_seq
            self.mod_map = lambda i, k: (mod_row0 + i // self.tiles_per_seq, 0, k)
        else:
            assert seq_len % 8 == 0 and n_seq * seq_len <= tile
            self.G, self.R = n_seq, seq_len
            self.tiles_per_seq = 1
            self.n_tiles = 1
            mod_blk = mod_row0 // n_seq
            assert mod_blk * n_seq == mod_row0
            self.mod_map = lambda i, k: (mod_blk, 0, k)
        self.tm = self.G * self.R
        self.n_rows = n_seq * seq_len

    def x_spec(self, d, n_grid=2, **kw):
        if n_grid == 2:
            return pl.BlockSpec((self.G, self.R, d), lambda i, j: (i, 0, 0), **kw)
        return pl.BlockSpec((self.G, self.R, d), lambda i: (i, 0, 0), **kw)

    def mod_spec(self, d, k, n_grid=2):
        if n_grid == 2:
            return pl.BlockSpec((self.G, 1, d), lambda i, j: self.mod_map(i, k))
        return pl.BlockSpec((self.G, 1, d), lambda i: self.mod_map(i, k))

    def view(self, x):
        return x.reshape(self.n_tiles * self.G, self.R, x.shape[-1])


def _inproj_kernel(x_ref, sh_ref, sc_ref, g_ref, w_ref, *rest, starts):
    o_refs, h_ref = rest[:-1], rest[-1]
    j = pl.program_id(1)

    @pl.when(j == 0)
    def _():
        h = _rms(x_ref[...], g_ref[...]) * (1.0 + sc_ref[...]) + sh_ref[...]
        h_ref[...] = h.reshape(h_ref.shape).astype(h_ref.dtype)

    y = _dot(h_ref[...], w_ref[...])
    for k, o_ref in enumerate(o_refs):
        @pl.when((j >= starts[k]) & (j < starts[k + 1]))
        def _(o_ref=o_ref):
            o_ref[...] = y


def _inproj(rows, x3, mod3, g_pre, w_in, widths, tn):
    d = x3.shape[-1]
    assert sum(widths) == w_in.shape[1] and all(w % tn == 0 for w in widths)
    starts = [0]
    for w in widths:
        starts.append(starts[-1] + w // tn)

    def out_spec(k):
        return pl.BlockSpec((rows.tm, tn),
                            lambda i, j: (i, jnp.clip(j - starts[k], 0, starts[k + 1] - starts[k] - 1)))

    return pl.pallas_call(
        functools.partial(_inproj_kernel, starts=tuple(starts)),
        grid=(rows.n_tiles, starts[-1]),
        in_specs=[rows.x_spec(d, pipeline_mode=pl.Buffered(1)),
                  rows.mod_spec(d, 0), rows.mod_spec(d, 1),
                  pl.BlockSpec((1, d), lambda i, j: (0, 0)),
                  pl.BlockSpec((d, tn), lambda i, j: (0, j))],
        out_specs=[out_spec(k) for k in range(len(widths))],
        out_shape=[jax.ShapeDtypeStruct((rows.n_rows, w), F32) for w in widths],
        scratch_shapes=[pltpu.VMEM((rows.tm, d), BF16)],
        compiler_params=_cparams(("parallel", "arbitrary")),
        name="inproj",
    )(x3, mod3, mod3, g_pre.reshape(1, d), w_in)


def _sb_logs(z, mask):
    t = jnp.log(1.0 + jnp.exp(-jnp.abs(z)))
    log_beta = jnp.minimum(z, 0.0) - t
    neg_stay = jnp.maximum(z, 0.0) + t
    if mask is not None:
        neg_stay = jnp.where(mask, neg_stay, 0.0)
    return log_beta, neg_stay


def _sb_weights(log_beta, neg_after, mask):
    w = jnp.exp(log_beta - neg_after)
    return w if mask is None else jnp.where(mask, w, 0.0)


def _tri(k):
    r = lax.broadcasted_iota(jnp.int32, (k, k), 0)
    c = lax.broadcasted_iota(jnp.int32, (k, k), 1)
    return (r > c).astype(BF16)


def _attn_prompt_kernel(bias_ref, tri_ref, q_ref, k_ref, v_ref, o_ref, c_ref, acc_ref, *, n_heads):
    blk = q_ref.shape[0]
    qi = pl.program_id(1)
    tri = tri_ref[...]

    def visit(start, mask, first):
        for h in range(n_heads):
            cols = slice(h * HEAD_DIM, (h + 1) * HEAD_DIM)
            q = q_ref[:, cols] * SB_SCALE
            k = k_ref[pl.ds(start, blk), cols]
            v = v_ref[pl.ds(start, blk), cols]
            log_beta, neg_stay = _sb_logs(_dot_nt(q, k) + bias_ref[h], mask)
            after = _dot(neg_stay.astype(BF16), tri)
            total = jnp.sum(neg_stay, axis=-1, keepdims=True)
            if first:
                c_ref[h] = total
            else:
                after = after + c_ref[h]
                c_ref[h] += total
            pv = _dot(_sb_weights(log_beta, after, mask), v)
            if first:
                acc_ref[h] = pv
            else:
                acc_ref[h] += pv

    r = lax.broadcasted_iota(jnp.int32, (blk, blk), 0)
    s = lax.broadcasted_iota(jnp.int32, (blk, blk), 1)
    visit(pl.multiple_of(qi * blk, blk), s < r, True)

    def body(it, _):
        visit(pl.multiple_of((qi - 1 - it) * blk, blk), None, False)
        return 0

    lax.fori_loop(0, qi, body, 0)
    for h in range(n_heads):
        o_ref[:, h * HEAD_DIM:(h + 1) * HEAD_DIM] = acc_ref[h]


def _attn_prompt(q, k, v, sb_bias, n_seq, seq_len, n_heads, blk=ATT_BLOCK):
    nq = seq_len // blk
    att_w = n_heads * HEAD_DIM
    return pl.pallas_call(
        functools.partial(_attn_prompt_kernel, n_heads=n_heads),
        grid=(n_seq, nq),
        in_specs=[pl.BlockSpec(memory_space=pltpu.SMEM),
                  pl.BlockSpec((blk, blk), lambda b, i: (0, 0)),
                  pl.BlockSpec((blk, att_w), lambda b, i: (b * nq + i, 0)),
                  pl.BlockSpec((seq_len, att_w), lambda b, i: (b, 0), pipeline_mode=pl.Buffered(1)),
                  pl.BlockSpec((seq_len, att_w), lambda b, i: (b, 0), pipeline_mode=pl.Buffered(1))],
        out_specs=pl.BlockSpec((blk, att_w), lambda b, i: (b * nq + i, 0)),
        out_shape=jax.ShapeDtypeStruct((n_seq * seq_len, att_w), F32),
        scratch_shapes=[pltpu.VMEM((n_heads, blk, 1), F32),
                        pltpu.VMEM((n_heads, blk, HEAD_DIM), F32)],
        compiler_params=_cparams(("parallel", "arbitrary")),
        name="attn_prompt",
    )(sb_bias, _tri(blk), q, k, v)


def _attn_sample_kernel(pt_ref, tri2_ref, bias_ref, q_ref, kn_ref, vn_ref, *rest, n_heads, t_new, pages):
    k_refs, v_refs = rest[:pages], rest[pages:2 * pages]
    o_ref, q2_ref, c_ref, acc_ref = rest[2 * pages:]
    s = pl.program_id(1)
    tri2 = tri2_ref[...]

    def head_masks(width):
        lane_head = lax.broadcasted_iota(jnp.int32, (t_new, width), 1) % n_heads
        return [lane_head == h for h in range(n_heads)]

    def lanes(x, j):
        return x[:, j * LANES:(j + 1) * LANES]

    def visit(kmats, vmats, bias, mask):
        hm = head_masks(kmats[0].shape[0])
        logs = []
        for km in kmats:
            zf = _dot_nt(q2_ref[...], km)
            z = zf[0:t_new]
            for h in range(1, n_heads):
                z = jnp.where(hm[h], zf[h * t_new:(h + 1) * t_new], z)
            logs.append(_sb_logs(z + bias, mask))
        n_ch = kmats[0].shape[0] // LANES
        order = [(i, j) for i in range(len(kmats)) for j in reversed(range(n_ch))]
        lhs = jnp.concatenate([lanes(logs[i][1], j) for i, j in order], axis=0)
        hi = lhs.astype(BF16)
        lo = (lhs - hi.astype(F32)).astype(BF16)
        st = _dot(hi, tri2) + _dot(lo, tri2)
        run = c_ref[...]
        after = {}
        for idx, ij in enumerate(order):
            blk = st[idx * t_new:(idx + 1) * t_new]
            after[ij] = blk[:, :LANES] + run
            run = run + blk[:, LANES:]
        c_ref[...] = run
        pv = None
        for i, vm in enumerate(vmats):
            aft = jnp.concatenate([after[(i, j)] for j in range(n_ch)], axis=1)
            w = _sb_weights(logs[i][0], aft, mask)
            we = jnp.concatenate([jnp.where(hm[h], w, 0.0) for h in range(n_heads)], axis=0)
            d = _dot(we, vm)
            pv = d if pv is None else pv + d
        acc_ref[...] += pv

    @pl.when(s == 0)
    def _():
        for h in range(n_heads):
            q2_ref[h * t_new:(h + 1) * t_new, :] = q_ref[:, h * HEAD_DIM:(h + 1) * HEAD_DIM] * SB_SCALE
        c_ref[...] = jnp.zeros_like(c_ref)
        acc_ref[...] = jnp.zeros_like(acc_ref)
        pad = jnp.zeros((LANES - kn_ref.shape[0], HEAD_DIM), F32)
        k_pos = lax.broadcasted_iota(jnp.int32, (t_new, LANES), 1) // n_heads
        q_pos = lax.broadcasted_iota(jnp.int32, (t_new, LANES), 0)
        visit([jnp.concatenate([kn_ref[...], pad], axis=0)], [jnp.concatenate([vn_ref[...], pad], axis=0)],
              bias_ref[:, :LANES], k_pos < q_pos)

    visit([r[...] for r in k_refs], [r[...] for r in v_refs], bias_ref[...], None)

    @pl.when(s == pl.num_programs(1) - 1)
    def _():
        for h in range(n_heads):
            o_ref[:, h * HEAD_DIM:(h + 1) * HEAD_DIM] = acc_ref[h * t_new:(h + 1) * t_new, :]


def _attn_sample(q, k, v, cache_k, cache_v, page_table, sb_bias, n_seq, t_new, n_heads, pages=PAGES_PER_STEP):
    n_phys, page = cache_k.shape[0], cache_k.shape[1]
    n_pages = page_table.shape[1]
    assert n_pages % pages == 0
    att_w = n_heads * HEAD_DIM
    rows = page * n_heads
    new_rows = t_new * n_heads
    assert LANES % n_heads == 0 and rows % LANES == 0 and new_rows <= LANES and t_new % 8 == 0
    ck = cache_k.reshape(n_phys, rows, HEAD_DIM)
    cv = cache_v.reshape(n_phys, rows, HEAD_DIM)
    kn = k.reshape(n_seq, new_rows, HEAD_DIM)
    vn = v.reshape(n_seq, new_rows, HEAD_DIM)
    bias = jnp.broadcast_to(jnp.tile(sb_bias, rows // n_heads)[None, :], (t_new, rows))
    r = lax.broadcasted_iota(jnp.int32, (LANES, LANES), 0)
    c = lax.broadcasted_iota(jnp.int32, (LANES, LANES), 1)
    same_head = (r % n_heads) == (c % n_heads)
    tri2 = jnp.concatenate([same_head & (r // n_heads > c // n_heads), same_head], axis=1).astype(BF16)

    def page_spec(i):
        return pl.BlockSpec((None, rows, HEAD_DIM),
                            lambda b, s, pt: (pt[b, n_pages - 1 - (s * pages + i)], 0, 0))

    grid_spec = pltpu.PrefetchScalarGridSpec(
        num_scalar_prefetch=1,
        grid=(n_seq, n_pages // pages),
        in_specs=[pl.BlockSpec((LANES, 2 * LANES), lambda b, s, pt: (0, 0)),
                  pl.BlockSpec((t_new, rows), lambda b, s, pt: (0, 0)),
                  pl.BlockSpec((t_new, att_w), lambda b, s, pt: (b, 0)),
                  pl.BlockSpec((None, new_rows, HEAD_DIM), lambda b, s, pt: (b, 0, 0)),
                  pl.BlockSpec((None, new_rows, HEAD_DIM), lambda b, s, pt: (b, 0, 0))]
                 + [page_spec(i) for i in range(pages)] * 2,
        out_specs=pl.BlockSpec((t_new, att_w), lambda b, s, pt: (b, 0)),
        scratch_shapes=[pltpu.VMEM((n_heads * t_new, HEAD_DIM), F32),
                        pltpu.VMEM((t_new, LANES), F32),
                        pltpu.VMEM((n_heads * t_new, HEAD_DIM), F32)],
    )
    kern = functools.partial(_attn_sample_kernel, n_heads=n_heads, t_new=t_new, pages=pages)
    return pl.pallas_call(
        kern,
        grid_spec=grid_spec,
        out_shape=jax.ShapeDtypeStruct((n_seq * t_new, att_w), F32),
        compiler_params=_cparams(("parallel", "arbitrary")),
        name="attn_sample",
    )(page_table, tri2, bias, q, kn, vn, *([ck] * pages), *([cv] * pages))


def _pool_kernel(u_ref, hist_ref, wg_ref, ps_ref, o_ref, ext_ref, pa_ref, pb_ref, *, G, R, tiles_per_seq,
                 past_len, fresh):
    i = pl.program_id(0)
    c_all = u_ref.shape[-1]
    cg = c_all // len(POOL_WINDOWS)
    n = POOL_HIST + R
    cur = u_ref[...].reshape(G, R, c_all)
    hist = hist_ref[...]
    t0 = 0
    if fresh:
        t0 = (i % tiles_per_seq) * R
        hist = jnp.where(t0 == 0, 0.0, hist)
    ext_ref[:, 0:POOL_HIST, :] = hist
    ext_ref[:, POOL_HIST:n, :] = cur

    src, dst = ext_ref, pa_ref
    for lvl, win in enumerate(POOL_WINDOWS):
        half = win // 2
        lo = lvl * cg
        rows_out = n - (win - 1)
        dst[:, 0:rows_out, lo:] = src[:, half:half + rows_out, lo:] + src[:, 0:rows_out, lo:]
        src, dst = dst, (pb_ref if dst is pa_ref else pa_ref)

    pos = past_len + t0 + lax.broadcasted_iota(jnp.int32, (1, R, cg), 1)
    for g, win in enumerate(POOL_WINDOWS):
        buf = pa_ref if g % 2 == 0 else pb_ref
        cols = slice(g * cg, (g + 1) * cg)
        first = POOL_HIST - (win - 1)
        wsum = buf[:, first:first + R, cols]
        cnt = jnp.minimum(win, pos + 1).astype(F32)
        pooled = wsum / cnt - cur[:, :, cols]
        mixed = _dot(pooled.reshape(G * R, cg), wg_ref[g])
        o_ref[:, cols] = mixed * ps_ref[:, cols]


def _pool(rows, u, hist3, hist_map, w_pool_group, pool_scale, past_len, fresh):
    n_groups, cg = w_pool_group.shape[0], w_pool_group.shape[1]
    pool_w = u.shape[1]
    G, R = rows.G, rows.R
    kern = functools.partial(_pool_kernel, G=G, R=R, tiles_per_seq=rows.tiles_per_seq,
                             past_len=past_len, fresh=fresh)
    ext = pltpu.VMEM((G, POOL_HIST + R, pool_w), F32)
    return pl.pallas_call(
        kern,
        grid=(rows.n_tiles,),
        in_specs=[pl.BlockSpec((rows.tm, pool_w), lambda i: (i, 0)),
                  pl.BlockSpec((G, POOL_HIST, pool_w), hist_map),
                  pl.BlockSpec((n_groups, cg, cg), lambda i: (0, 0, 0)),
                  pl.BlockSpec((1, pool_w), lambda i: (0, 0))],
        out_specs=pl.BlockSpec((rows.tm, pool_w), lambda i: (i, 0)),
        out_shape=jax.ShapeDtypeStruct((rows.n_rows, pool_w), F32),
        scratch_shapes=[ext, ext, ext],
        compiler_params=_cparams(("parallel",)),
        name="pool",
    )(u, hist3, w_pool_group, pool_scale.reshape(1, pool_w))


def _merge_kernel(a_ref, p_ref, gla_ref, glp_ref, bga_ref, bgp_ref, wa_ref, wp_ref, o_ref):
    ga = jax.nn.sigmoid(gla_ref[...] + bga_ref[...])
    gp = jax.nn.sigmoid(glp_ref[...] + bgp_ref[...])
    o_ref[...] = ga * _dot(a_ref[...], wa_ref[...]) + gp * _dot(p_ref[...], wp_ref[...])


def _merge(rows, a, p, gl, b_gate, w_a, w_p, tn=512):
    d = w_a.shape[1]
    tm = rows.tm
    nd = d // tn
    return pl.pallas_call(
        _merge_kernel,
        grid=(rows.n_tiles, nd),
        in_specs=[pl.BlockSpec((tm, a.shape[1]), lambda i, j: (i, 0)),
                  pl.BlockSpec((tm, p.shape[1]), lambda i, j: (i, 0)),
                  pl.BlockSpec((tm, tn), lambda i, j: (i, j)),
                  pl.BlockSpec((tm, tn), lambda i, j: (i, nd + j)),
                  pl.BlockSpec((1, tn), lambda i, j: (0, j)),
                  pl.BlockSpec((1, tn), lambda i, j: (0, nd + j)),
                  pl.BlockSpec((w_a.shape[0], tn), lambda i, j: (0, j)),
                  pl.BlockSpec((w_p.shape[0], tn), lambda i, j: (0, j))],
        out_specs=pl.BlockSpec((tm, tn), lambda i, j: (i, j)),
        out_shape=jax.ShapeDtypeStruct((rows.n_rows, d), F32),
        compiler_params=_cparams(("parallel", "arbitrary")),
        name="merge",
    )(a, p, gl, gl, b_gate.reshape(1, 2 * d), b_gate.reshape(1, 2 * d), w_a, w_p)


def _outproj_kernel(m_ref, w_ref, g_ref, x_ref, gt_ref, o_ref):
    y = _dot(m_ref[...], w_ref[...])
    o_ref[...] = x_ref[...] + gt_ref[...] * _rms(y, g_ref[...]).reshape(x_ref.shape)


def _outproj(rows, merged, w_out, g_post, x3, mod3):
    d = w_out.shape[1]
    return pl.pallas_call(
        _outproj_kernel,
        grid=(rows.n_tiles,),
        in_specs=[pl.BlockSpec((rows.tm, d), lambda i: (i, 0)),
                  pl.BlockSpec((d, d), lambda i: (0, 0), pipeline_mode=pl.Buffered(1)),
                  pl.BlockSpec((1, d), lambda i: (0, 0)),
                  rows.x_spec(d, n_grid=1),
                  rows.mod_spec(d, 2, n_grid=1)],
        out_specs=rows.x_spec(d, n_grid=1),
        out_shape=jax.ShapeDtypeStruct(x3.shape, F32),
        compiler_params=_cparams(("parallel",)),
        name="outproj",
    )(merged, w_out, g_post.reshape(1, d), x3, mod3)


def _ffn_kernel(x_ref, sh_ref, sc_ref, gt_ref, gpre_ref, gpost_ref, wg_ref, wu_ref, wo_ref, o_ref, h_ref, acc_ref):
    j = pl.program_id(1)

    @pl.when(j == 0)
    def _():
        h = _rms(x_ref[...], gpre_ref[...]) * (1.0 + sc_ref[...]) + sh_ref[...]
        h_ref[...] = h.reshape(h_ref.shape).astype(h_ref.dtype)
        acc_ref[...] = jnp.zeros_like(acc_ref)

    h = h_ref[...]
    act = jax.nn.silu(_dot(h, wg_ref[...])) * _dot(h, wu_ref[...])
    acc_ref[...] += _dot(act, wo_ref[...])

    @pl.when(j == pl.num_programs(1) - 1)
    def _():
        o_ref[...] = x_ref[...] + gt_ref[...] * _rms(acc_ref[...], gpost_ref[...]).reshape(x_ref.shape)


def _ffn(rows, x3, mod3, g_pre, g_post, w_ffn_in, w_ffn_out, tf=256):
    d = x3.shape[-1]
    d_ff = w_ffn_out.shape[0]
    nf = d_ff // tf
    assert nf * tf == d_ff
    return pl.pallas_call(
        _ffn_kernel,
        grid=(rows.n_tiles, nf),
        in_specs=[rows.x_spec(d, pipeline_mode=pl.Buffered(1)),
                  rows.mod_spec(d, 3), rows.mod_spec(d, 4), rows.mod_spec(d, 5),
                  pl.BlockSpec((1, d), lambda i, j: (0, 0)),
                  pl.BlockSpec((1, d), lambda i, j: (0, 0)),
                  pl.BlockSpec((d, tf), lambda i, j: (0, j)),
                  pl.BlockSpec((d, tf), lambda i, j: (0, nf + j)),
                  pl.BlockSpec((tf, d), lambda i, j: (j, 0))],
        out_specs=rows.x_spec(d),
        out_shape=jax.ShapeDtypeStruct(x3.shape, F32),
        scratch_shapes=[pltpu.VMEM((rows.tm, d), BF16), pltpu.VMEM((rows.tm, d), F32)],
        compiler_params=_cparams(("parallel", "arbitrary")),
        name="ffn",
    )(x3, mod3, mod3, mod3, g_pre.reshape(1, d), g_post.reshape(1, d), w_ffn_in, w_ffn_in, w_ffn_out)


def _group_layer(mk_rows, x, mod3, attn_fn, hist_fn, past_len, fresh, lw, n_heads, cols):
    (g_mix_pre, g_mix_post, g_ffn_pre, g_ffn_post, w_in, b_gate, w_pool_group, pool_scale,
     w_branch_att, w_branch_pool, w_out, w_ffn_in, w_ffn_out) = lw
    n_seq, seq_len, d = x.shape
    att_w = n_heads * HEAD_DIM
    pool_w = pool_scale.shape[0]
    rows = mk_rows(TILE_INPROJ)
    q, k, v, u, gl = _inproj(rows, rows.view(x), mod3, g_mix_pre, w_in,
                             widths=(att_w, att_w, att_w, pool_w, w_in.shape[1] - 3 * att_w - pool_w),
                             tn=cols[0])
    a = attn_fn(q, k, v)
    rows = mk_rows(TILE_POOL)
    hist3, hist_map = hist_fn(rows, u)
    p = _pool(rows, u, hist3, hist_map, w_pool_group, pool_scale, past_len=past_len, fresh=fresh)
    merged = _merge(mk_rows(TILE_MERGE), a, p, gl, b_gate, w_branch_att, w_branch_pool, tn=cols[1])
    rows = mk_rows(TILE_OUTPROJ)
    x1 = _outproj(rows, merged, w_out, g_mix_post, rows.view(x), mod3)
    rows = mk_rows(TILE_FFN)
    x2 = _ffn(rows, rows.view(x1), mod3, g_ffn_pre, g_ffn_post, w_ffn_in, w_ffn_out, tf=cols[2])
    return (x2.reshape(n_seq, seq_len, d), k.reshape(n_seq, seq_len, n_heads, HEAD_DIM),
            v.reshape(n_seq, seq_len, n_heads, HEAD_DIM), u.reshape(n_seq, seq_len, pool_w))


def kernel(x_prompt, x_sample, c_prompt, c_sample, cache_k, cache_v, page_table, state_pool, w_cond, b_cond, g_mix_pre, g_mix_post, g_ffn_pre, g_ffn_post, w_in, b_gate, sb_bias, w_pool_group, pool_scale, w_branch_att, w_branch_pool, w_out, w_ffn_in, w_ffn_out):
    depth = w_cond.shape[0]
    b_p, t_p, d = x_prompt.shape
    b_s, t_s, _ = x_sample.shape
    n_heads = sb_bias.shape[1]
    pool_w = pool_scale.shape[1]
    past_len = page_table.shape[1] * cache_k.shape[2]
    pad = (-(b_s + b_p)) % 8
    c_all = jnp.concatenate([c_sample, c_prompt, jnp.zeros((pad, d), F32)], axis=0)

    yp, ys = x_prompt, x_sample
    outs = [[] for _ in range(6)]
    for l in range(depth):
        lw = (g_mix_pre[l], g_mix_post[l], g_ffn_pre[l], g_ffn_post[l], w_in[l], b_gate[l], w_pool_group[l],
              pool_scale[l], w_branch_att[l], w_branch_pool[l], w_out[l], w_ffn_in[l], w_ffn_out[l])
        mod = _modulation(c_all, w_cond[l], b_cond[l])
        mod3 = mod.reshape(mod.shape[0], 1, mod.shape[1])

        def hist_prompt(rows, u):
            per_tile = rows.R // POOL_HIST
            hist3 = u.reshape(u.shape[0] // POOL_HIST, POOL_HIST, u.shape[1])
            return hist3, lambda i: (jnp.maximum(i * per_tile - 1, 0), 0, 0)

        yp, kp, vp, up = _group_layer(
            lambda tile: _Rows(b_p, t_p, tile, mod_row0=b_s), yp, mod3,
            functools.partial(_attn_prompt, sb_bias=sb_bias[l], n_seq=b_p, seq_len=t_p, n_heads=n_heads),
            hist_prompt, past_len=0, fresh=True, lw=lw, n_heads=n_heads, cols=COLS_LONG)
        bp = up[:, t_p - POOL_BUF:, :]

        hist_s = jnp.pad(state_pool[l], ((0, 0), (POOL_HIST - POOL_BUF, 0), (0, 0)))
        ys, ksn, vsn, us = _group_layer(
            lambda tile: _Rows(b_s, t_s, tile, mod_row0=0), ys, mod3,
            functools.partial(_attn_sample, cache_k=cache_k[l], cache_v=cache_v[l], page_table=page_table,
                              sb_bias=sb_bias[l], n_seq=b_s, t_new=t_s, n_heads=n_heads),
            lambda rows, u: (hist_s, lambda i: (0, 0, 0)),
            past_len=past_len, fresh=False, lw=lw, n_heads=n_heads, cols=COLS_SHORT)
        bsn = jnp.concatenate([state_pool[l], us], axis=1)[:, -POOL_BUF:, :]

        for lst, val in zip(outs, (kp, vp, bp, ksn, vsn, bsn)):
            lst.append(val)
    return (yp, ys) + tuple(jnp.stack(o) for o in outs)
```

```python
import functools

import jax
import jax.numpy as jnp
from jax import lax
from jax.experimental import pallas as pl
from jax.experimental.pallas import tpu as pltpu

F32 = jnp.float32
BF16 = jnp.bfloat16

HEAD_DIM = 128
LANES = 128
POOL_WINDOWS = (2, 4, 8, 16)
POOL_BUF = max(POOL_WINDOWS) - 1
POOL_HIST = POOL_BUF + 1
EPS = 1e-6
SB_SCALE = HEAD_DIM ** -0.5

VMEM_LIMIT_BYTES = 60 * 1024 * 1024
ATT_BLOCK = 256
PAGES_PER_STEP = 16
TILE_INPROJ = 1024
TILE_MIXER = 256
TILE_FFN = 1024
COLS_LONG = (512, 512)
COLS_SHORT = (1024, 512)


def _cparams(semantics):
    return pltpu.CompilerParams(dimension_semantics=semantics, vmem_limit_bytes=VMEM_LIMIT_BYTES)


def _dot(a, b):
    return jnp.dot(a, b, preferred_element_type=F32)


def _dot_nt(a, b):
    return lax.dot_general(a, b, (((1,), (1,)), ((), ())), preferred_element_type=F32)


def _rms(x, g):
    return x * lax.rsqrt(jnp.mean(x * x, axis=-1, keepdims=True) + EPS) * g


def _mod_kernel(c_ref, w_ref, b_ref, o_ref):
    o_ref[...] = _dot(jax.nn.silu(c_ref[...]), w_ref[...]) + b_ref[...]


def _modulation(c_all, w_cond, b_cond, tn=1024):
    m, d = c_all.shape
    n = w_cond.shape[1]
    return pl.pallas_call(
        _mod_kernel,
        grid=(n // tn,),
        in_specs=[pl.BlockSpec((m, d), lambda j: (0, 0)),
                  pl.BlockSpec((d, tn), lambda j: (0, j)),
                  pl.BlockSpec((1, tn), lambda j: (0, j))],
        out_specs=pl.BlockSpec((m, tn), lambda j: (0, j)),
        out_shape=jax.ShapeDtypeStruct((m, n), F32),
        compiler_params=_cparams(("arbitrary",)),
        name="modulation",
    )(c_all, w_cond, b_cond.reshape(1, n))


class _Rows:
    def __init__(self, n_seq, seq_len, tile, mod_row0):
        if seq_len >= tile:
            assert seq_len % tile == 0
            self.G, self.R = 1, tile
            self.tiles_per_seq = seq_len // tile
            self.n_tiles = n_seq * self.tiles_per_seq
            self.mod_map = lambda i, k: (mod_row0 + i // self.tiles_per_seq, 0, k)
        else:
            assert seq_len % 8 == 0 and n_seq * seq_len <= tile
            self.G, self.R = n_seq, seq_len
            self.tiles_per_seq = 1
            self.n_tiles = 1
            mod_blk = mod_row0 // n_seq
            assert mod_blk * n_seq == mod_row0
            self.mod_map = lambda i, k: (mod_blk, 0, k)
        self.tm = self.G * self.R
        self.n_rows = n_seq * seq_len

    def x_spec(self, d, n_grid=2, **kw):
        if n_grid == 2:
            return pl.BlockSpec((self.G, self.R, d), lambda i, j: (i, 0, 0), **kw)
        return pl.BlockSpec((self.G, self.R, d), lambda i: (i, 0, 0), **kw)

    def mod_spec(self, d, k, n_grid=2):
        if n_grid == 2:
            return pl.BlockSpec((self.G, 1, d), lambda i, j: self.mod_map(i, k))
        return pl.BlockSpec((self.G, 1, d), lambda i: self.mod_map(i, k))

    def view(self, x):
        return x.reshape(self.n_tiles * self.G, self.R, x.shape[-1])


def _inproj_kernel(x_ref, sh_ref, sc_ref, g_ref, w_ref, *rest, starts, emit):
    n_out = len(starts) - 1
    o_refs, h_ref = rest[:n_out], rest[-1]
    j = pl.program_id(1)

    @pl.when(j == 0)
    def _():
        h = _rms(x_ref[...], g_ref[...]) * (1.0 + sc_ref[...]) + sh_ref[...]
        h_ref[...] = h.reshape(h_ref.shape).astype(h_ref.dtype)

    if emit:
        rest[n_out][...] = w_ref[...].astype(BF16)
    for k, o_ref in enumerate(o_refs):
        @pl.when((j >= starts[k]) & (j < starts[k + 1]))
        def _(o_ref=o_ref):
            o_ref[...] = _dot(h_ref[...], w_ref[...].astype(BF16))


def _inproj(rows, x3, mod3, g_pre, w_in, widths, tn, emit):
    d = x3.shape[-1]
    assert sum(widths) == w_in.shape[1] and all(w % tn == 0 for w in widths)
    starts = [0]
    for w in widths:
        starts.append(starts[-1] + w // tn)

    def out_spec(k):
        return pl.BlockSpec((rows.tm, tn),
                            lambda i, j: (i, jnp.clip(j - starts[k], 0, starts[k + 1] - starts[k] - 1)))

    w_spec = pl.BlockSpec((d, tn), lambda i, j: (0, j))
    return pl.pallas_call(
        functools.partial(_inproj_kernel, starts=tuple(starts), emit=emit),
        grid=(rows.n_tiles, starts[-1]),
        in_specs=[rows.x_spec(d, pipeline_mode=pl.Buffered(1)),
                  rows.mod_spec(d, 0), rows.mod_spec(d, 1),
                  pl.BlockSpec((1, d), lambda i, j: (0, 0)),
                  w_spec],
        out_specs=[out_spec(k) for k in range(len(widths))] + ([w_spec] if emit else []),
        out_shape=[jax.ShapeDtypeStruct((rows.n_rows, w), F32) for w in widths]
                  + ([jax.ShapeDtypeStruct(w_in.shape, BF16)] if emit else []),
        scratch_shapes=[pltpu.VMEM((rows.tm, d), BF16)],
        compiler_params=_cparams(("parallel", "arbitrary")),
        name="inproj",
    )(x3, mod3, mod3, g_pre.reshape(1, d), w_in)


def _sb_logs(z, mask):
    t = jnp.log(1.0 + jnp.exp(-jnp.abs(z)))
    log_beta = jnp.minimum(z, 0.0) - t
    neg_stay = jnp.maximum(z, 0.0) + t
    if mask is not None:
        neg_stay = jnp.where(mask, neg_stay, 0.0)
    return log_beta, neg_stay


def _sb_weights(log_beta, neg_after, mask):
    w = jnp.exp(log_beta - neg_after)
    return w if mask is None else jnp.where(mask, w, 0.0)


def _tri(k):
    r = lax.broadcasted_iota(jnp.int32, (k, k), 0)
    c = lax.broadcasted_iota(jnp.int32, (k, k), 1)
    return (r > c).astype(BF16)


def _attn_prompt_kernel(bias_ref, tri_ref, q_ref, k_ref, v_ref, o_ref, c_ref, acc_ref, *, n_heads):
    blk = q_ref.shape[0]
    qi = pl.program_id(1)
    tri = tri_ref[...]

    def visit(start, mask, first):
        for h in range(n_heads):
            cols = slice(h * HEAD_DIM, (h + 1) * HEAD_DIM)
            q = q_ref[:, cols] * SB_SCALE
            k = k_ref[pl.ds(start, blk), cols]
            v = v_ref[pl.ds(start, blk), cols]
            log_beta, neg_stay = _sb_logs(_dot_nt(q, k) + bias_ref[h], mask)
            after = _dot(neg_stay.astype(BF16), tri)
            total = jnp.sum(neg_stay, axis=-1, keepdims=True)
            if first:
                c_ref[h] = total
            else:
                after = after + c_ref[h]
                c_ref[h] += total
            pv = _dot(_sb_weights(log_beta, after, mask), v)
            if first:
                acc_ref[h] = pv
            else:
                acc_ref[h] += pv

    r = lax.broadcasted_iota(jnp.int32, (blk, blk), 0)
    s = lax.broadcasted_iota(jnp.int32, (blk, blk), 1)
    visit(pl.multiple_of(qi * blk, blk), s < r, True)

    def body(it, _):
        visit(pl.multiple_of((qi - 1 - it) * blk, blk), None, False)
        return 0

    lax.fori_loop(0, qi, body, 0)
    for h in range(n_heads):
        o_ref[:, h * HEAD_DIM:(h + 1) * HEAD_DIM] = acc_ref[h]


def _attn_prompt(q, k, v, sb_bias, n_seq, seq_len, n_heads, blk=ATT_BLOCK):
    nq = seq_len // blk
    att_w = n_heads * HEAD_DIM
    return pl.pallas_call(
        functools.partial(_attn_prompt_kernel, n_heads=n_heads),
        grid=(n_seq, nq),
        in_specs=[pl.BlockSpec(memory_space=pltpu.SMEM),
                  pl.BlockSpec((blk, blk), lambda b, i: (0, 0)),
                  pl.BlockSpec((blk, att_w), lambda b, i: (b * nq + i, 0)),
                  pl.BlockSpec((seq_len, att_w), lambda b, i: (b, 0), pipeline_mode=pl.Buffered(1)),
                  pl.BlockSpec((seq_len, att_w), lambda b, i: (b, 0), pipeline_mode=pl.Buffered(1))],
        out_specs=pl.BlockSpec((blk, att_w), lambda b, i: (b * nq + i, 0)),
        out_shape=jax.ShapeDtypeStruct((n_seq * seq_len, att_w), F32),
        scratch_shapes=[pltpu.VMEM((n_heads, blk, 1), F32),
                        pltpu.VMEM((n_heads, blk, HEAD_DIM), F32)],
        compiler_params=_cparams(("parallel", "arbitrary")),
        name="attn_prompt",
    )(sb_bias, _tri(blk), q, k, v)


def _attn_sample_kernel(pt_ref, tri2_ref, bias_ref, q_ref, kn_ref, vn_ref, *rest, n_heads, t_new, pages):
    k_refs, v_refs = rest[:pages], rest[pages:2 * pages]
    o_ref, q2_ref, c_ref, acc_ref = rest[2 * pages:]
    s = pl.program_id(1)
    tri2 = tri2_ref[...]

    def head_masks(width):
        lane_head = lax.broadcasted_iota(jnp.int32, (t_new, width), 1) % n_heads
        return [lane_head == h for h in range(n_heads)]

    def lanes(x, j):
        return x[:, j * LANES:(j + 1) * LANES]

    def visit(kmats, vmats, bias, mask):
        hm = head_masks(kmats[0].shape[0])
        logs = []
        for km in kmats:
            zf = _dot_nt(q2_ref[...], km)
            z = zf[0:t_new]
            for h in range(1, n_heads):
                z = jnp.where(hm[h], zf[h * t_new:(h + 1) * t_new], z)
            logs.append(_sb_logs(z + bias, mask))
        n_ch = kmats[0].shape[0] // LANES
        order = [(i, j) for i in range(len(kmats)) for j in reversed(range(n_ch))]
        lhs = jnp.concatenate([lanes(logs[i][1], j) for i, j in order], axis=0)
        hi = lhs.astype(BF16)
        lo = (lhs - hi.astype(F32)).astype(BF16)
        st = _dot(hi, tri2) + _dot(lo, tri2)
        run = c_ref[...]
        after = {}
        for idx, ij in enumerate(order):
            blk = st[idx * t_new:(idx + 1) * t_new]
            after[ij] = blk[:, :LANES] + run
            run = run + blk[:, LANES:]
        c_ref[...] = run
        pv = None
        for i, vm in enumerate(vmats):
            aft = jnp.concatenate([after[(i, j)] for j in range(n_ch)], axis=1)
            w = _sb_weights(logs[i][0], aft, mask)
            we = jnp.concatenate([jnp.where(hm[h], w, 0.0) for h in range(n_heads)], axis=0)
            d = _dot(we, vm)
            pv = d if pv is None else pv + d
        acc_ref[...] += pv

    @pl.when(s == 0)
    def _():
        for h in range(n_heads):
            q2_ref[h * t_new:(h + 1) * t_new, :] = q_ref[:, h * HEAD_DIM:(h + 1) * HEAD_DIM] * SB_SCALE
        c_ref[...] = jnp.zeros_like(c_ref)
        acc_ref[...] = jnp.zeros_like(acc_ref)
        pad = jnp.zeros((LANES - kn_ref.shape[0], HEAD_DIM), F32)
        k_pos = lax.broadcasted_iota(jnp.int32, (t_new, LANES), 1) // n_heads
        q_pos = lax.broadcasted_iota(jnp.int32, (t_new, LANES), 0)
        visit([jnp.concatenate([kn_ref[...], pad], axis=0)], [jnp.concatenate([vn_ref[...], pad], axis=0)],
              bias_ref[:, :LANES], k_pos < q_pos)

    visit([r[...] for r in k_refs], [r[...] for r in v_refs], bias_ref[...], None)

    @pl.when(s == pl.num_programs(1) - 1)
    def _():
        for h in range(n_heads):
            o_ref[:, h * HEAD_DIM:(h + 1) * HEAD_DIM] = acc_ref[h * t_new:(h + 1) * t_new, :]


def _attn_sample(q, k, v, cache_k, cache_v, page_table, sb_bias, n_seq, t_new, n_heads, pages=PAGES_PER_STEP):
    n_phys, page = cache_k.shape[0], cache_k.shape[1]
    n_pages = page_table.shape[1]
    assert n_pages % pages == 0
    att_w = n_heads * HEAD_DIM
    rows = page * n_heads
    new_rows = t_new * n_heads
    assert LANES % n_heads == 0 and rows % LANES == 0 and new_rows <= LANES and t_new % 8 == 0
    ck = cache_k.reshape(n_phys, rows, HEAD_DIM)
    cv = cache_v.reshape(n_phys, rows, HEAD_DIM)
    kn = k.reshape(n_seq, new_rows, HEAD_DIM)
    vn = v.reshape(n_seq, new_rows, HEAD_DIM)
    bias = jnp.broadcast_to(jnp.tile(sb_bias, rows // n_heads)[None, :], (t_new, rows))
    r = lax.broadcasted_iota(jnp.int32, (LANES, LANES), 0)
    c = lax.broadcasted_iota(jnp.int32, (LANES, LANES), 1)
    same_head = (r % n_heads) == (c % n_heads)
    tri2 = jnp.concatenate([same_head & (r // n_heads > c // n_heads), same_head], axis=1).astype(BF16)

    def page_spec(i):
        return pl.BlockSpec((None, rows, HEAD_DIM),
                            lambda b, s, pt: (pt[b, n_pages - 1 - (s * pages + i)], 0, 0))

    grid_spec = pltpu.PrefetchScalarGridSpec(
        num_scalar_prefetch=1,
        grid=(n_seq, n_pages // pages),
        in_specs=[pl.BlockSpec((LANES, 2 * LANES), lambda b, s, pt: (0, 0)),
                  pl.BlockSpec((t_new, rows), lambda b, s, pt: (0, 0)),
                  pl.BlockSpec((t_new, att_w), lambda b, s, pt: (b, 0)),
                  pl.BlockSpec((None, new_rows, HEAD_DIM), lambda b, s, pt: (b, 0, 0)),
                  pl.BlockSpec((None, new_rows, HEAD_DIM), lambda b, s, pt: (b, 0, 0))]
                 + [page_spec(i) for i in range(pages)] * 2,
        out_specs=pl.BlockSpec((t_new, att_w), lambda b, s, pt: (b, 0)),
        scratch_shapes=[pltpu.VMEM((n_heads * t_new, HEAD_DIM), F32),
                        pltpu.VMEM((t_new, LANES), F32),
                        pltpu.VMEM((n_heads * t_new, HEAD_DIM), F32)],
    )
    kern = functools.partial(_attn_sample_kernel, n_heads=n_heads, t_new=t_new, pages=pages)
    return pl.pallas_call(
        kern,
        grid_spec=grid_spec,
        out_shape=jax.ShapeDtypeStruct((n_seq * t_new, att_w), F32),
        compiler_params=_cparams(("parallel", "arbitrary")),
        name="attn_sample",
    )(page_table, tri2, bias, q, kn, vn, *([ck] * pages), *([cv] * pages))


def _pool_mix(u_ref, hist_ref, wg_ref, ps_ref, p_ref, ext_ref, pa_ref, pb_ref, *, G, R, t0, past_len, fresh):
    c_all = u_ref.shape[-1]
    cg = c_all // len(POOL_WINDOWS)
    n = POOL_HIST + R
    cur = u_ref[...].reshape(G, R, c_all)
    hist = hist_ref[...]
    if fresh:
        hist = jnp.where(t0 == 0, 0.0, hist)
    ext_ref[:, 0:POOL_HIST, :] = hist
    ext_ref[:, POOL_HIST:n, :] = cur

    src, dst = ext_ref, pa_ref
    for lvl, win in enumerate(POOL_WINDOWS):
        half = win // 2
        lo = lvl * cg
        rows_out = n - (win - 1)
        dst[:, 0:rows_out, lo:] = src[:, half:half + rows_out, lo:] + src[:, 0:rows_out, lo:]
        src, dst = dst, (pb_ref if dst is pa_ref else pa_ref)

    pos = past_len + t0 + lax.broadcasted_iota(jnp.int32, (1, R, cg), 1)
    for g, win in enumerate(POOL_WINDOWS):
        buf = pa_ref if g % 2 == 0 else pb_ref
        cols = slice(g * cg, (g + 1) * cg)
        first = POOL_HIST - (win - 1)
        wsum = buf[:, first:first + R, cols]
        cnt = jnp.minimum(win, pos + 1).astype(F32)
        pooled = wsum / cnt - cur[:, :, cols]
        mixed = _dot(pooled.reshape(G * R, cg), wg_ref[g])
        p_ref[:, cols] = (mixed * ps_ref[:, cols]).astype(p_ref.dtype)


def _mixer_kernel(a_ref, u_ref, hist_ref, gl_ref, x_ref, gt_ref, wg_ref, ps_ref, bg_ref, wa_ref, wp_ref, wo_ref,
                  gpost_ref, o_ref, p_ref, ext_ref, pa_ref, pb_ref, *, G, R, tiles_per_seq, past_len, fresh):
    d = x_ref.shape[-1]
    t0 = (pl.program_id(0) % tiles_per_seq) * R if fresh else 0
    _pool_mix(u_ref, hist_ref, wg_ref, ps_ref, p_ref, ext_ref, pa_ref, pb_ref,
              G=G, R=R, t0=t0, past_len=past_len, fresh=fresh)
    ga = jax.nn.sigmoid(gl_ref[:, :d] + bg_ref[:, :d])
    gp = jax.nn.sigmoid(gl_ref[:, d:] + bg_ref[:, d:])
    merged = ga * _dot(a_ref[...].astype(BF16), wa_ref[...]) + gp * _dot(p_ref[...], wp_ref[...])
    y = _dot(merged.astype(BF16), wo_ref[...])
    o_ref[...] = x_ref[...] + gt_ref[...] * _rms(y, gpost_ref[...]).reshape(x_ref.shape)


def _mixer(rows, a, u, hist3, hist_map, gl, x3, mod3, w_pool_group, pool_scale, b_gate, w_a, w_p, w_o, g_post,
           past_len, fresh):
    d = x3.shape[-1]
    n_groups, cg = w_pool_group.shape[0], w_pool_group.shape[1]
    pool_w, att_w = u.shape[1], a.shape[1]
    G, R, tm = rows.G, rows.R, rows.tm
    const = lambda shape: pl.BlockSpec(shape, lambda i: (0,) * len(shape), pipeline_mode=pl.Buffered(1))
    ext = pltpu.VMEM((G, POOL_HIST + R, pool_w), F32)
    kern = functools.partial(_mixer_kernel, G=G, R=R, tiles_per_seq=rows.tiles_per_seq,
                             past_len=past_len, fresh=fresh)
    return pl.pallas_call(
        kern,
        grid=(rows.n_tiles,),
        in_specs=[pl.BlockSpec((tm, att_w), lambda i: (i, 0)),
                  pl.BlockSpec((tm, pool_w), lambda i: (i, 0)),
                  pl.BlockSpec((G, POOL_HIST, pool_w), hist_map),
                  pl.BlockSpec((tm, 2 * d), lambda i: (i, 0)),
                  rows.x_spec(d, n_grid=1),
                  rows.mod_spec(d, 2, n_grid=1),
                  const((n_groups, cg, cg)), const((1, pool_w)), const((1, 2 * d)),
                  const((att_w, d)), const((pool_w, d)), const((d, d)), const((1, d))],
        out_specs=rows.x_spec(d, n_grid=1),
        out_shape=jax.ShapeDtypeStruct(x3.shape, F32),
        scratch_shapes=[pltpu.VMEM((tm, pool_w), BF16), ext, ext, ext],
        compiler_params=_cparams(("parallel",)),
        name="mixer",
    )(a, u, hist3, gl, x3, mod3, w_pool_group, pool_scale.reshape(1, pool_w), b_gate.reshape(1, 2 * d),
      w_a, w_p, w_o, g_post.reshape(1, d))


def _ffn_kernel(x_ref, sh_ref, sc_ref, gt_ref, gpre_ref, gpost_ref, wg_ref, wu_ref, wo_ref, o_ref, *rest, emit):
    h_ref = rest[-1]
    j = pl.program_id(1)

    @pl.when(j == 0)
    def _():
        h = _rms(x_ref[...], gpre_ref[...]) * (1.0 + sc_ref[...]) + sh_ref[...]
        h_ref[...] = h.reshape(h_ref.shape).astype(h_ref.dtype)
        o_ref[...] = jnp.zeros_like(o_ref)

    wg, wu, wo = (r[...].astype(BF16) for r in (wg_ref, wu_ref, wo_ref))
    if emit:
        for ref, val in zip(rest[:3], (wg, wu, wo)):
            ref[...] = val
    h = h_ref[...]
    act = jax.nn.silu(_dot(h, wg)) * _dot(h, wu)
    o_ref[...] += _dot(act.astype(BF16), wo).reshape(o_ref.shape)

    @pl.when(j == pl.num_programs(1) - 1)
    def _():
        o_ref[...] = x_ref[...] + gt_ref[...] * _rms(o_ref[...], gpost_ref[...])


def _ffn(rows, x3, mod3, g_pre, g_post, wg, wu, wo, wu_col0, tf, emit):
    d = x3.shape[-1]
    d_ff = wo.shape[0]
    nf = d_ff // tf
    assert nf * tf == d_ff and wu_col0 % tf == 0
    u0 = wu_col0 // tf
    w_specs = [pl.BlockSpec((d, tf), lambda i, j: (0, j)),
               pl.BlockSpec((d, tf), lambda i, j: (0, u0 + j)),
               pl.BlockSpec((tf, d), lambda i, j: (j, 0))]
    emit_specs = [pl.BlockSpec((d, tf), lambda i, j: (0, j)),
                  pl.BlockSpec((d, tf), lambda i, j: (0, j)),
                  pl.BlockSpec((tf, d), lambda i, j: (j, 0))]
    emit_shapes = [jax.ShapeDtypeStruct((d, d_ff), BF16), jax.ShapeDtypeStruct((d, d_ff), BF16),
                   jax.ShapeDtypeStruct((d_ff, d), BF16)]
    return pl.pallas_call(
        functools.partial(_ffn_kernel, emit=emit),
        grid=(rows.n_tiles, nf),
        in_specs=[rows.x_spec(d, pipeline_mode=pl.Buffered(1)),
                  rows.mod_spec(d, 3), rows.mod_spec(d, 4), rows.mod_spec(d, 5),
                  pl.BlockSpec((1, d), lambda i, j: (0, 0)),
                  pl.BlockSpec((1, d), lambda i, j: (0, 0))] + w_specs,
        out_specs=[rows.x_spec(d)] + (emit_specs if emit else []),
        out_shape=[jax.ShapeDtypeStruct(x3.shape, F32)] + (emit_shapes if emit else []),
        scratch_shapes=[pltpu.VMEM((rows.tm, d), BF16)],
        compiler_params=_cparams(("parallel", "arbitrary")),
        name="ffn",
    )(x3, mod3, mod3, mod3, g_pre.reshape(1, d), g_post.reshape(1, d), wg, wu, wo)


def kernel(x_prompt, x_sample, c_prompt, c_sample, cache_k, cache_v, page_table, state_pool, w_cond, b_cond, g_mix_pre, g_mix_post, g_ffn_pre, g_ffn_post, w_in, b_gate, sb_bias, w_pool_group, pool_scale, w_branch_att, w_branch_pool, w_out, w_ffn_in, w_ffn_out):
    depth = w_cond.shape[0]
    b_p, t_p, d = x_prompt.shape
    b_s, t_s, _ = x_sample.shape
    n_heads = sb_bias.shape[1]
    att_w = n_heads * HEAD_DIM
    pool_w = pool_scale.shape[1]
    d_ff = w_ffn_out.shape[1]
    past_len = page_table.shape[1] * cache_k.shape[2]
    widths = (att_w, att_w, att_w, pool_w, w_in.shape[2] - 3 * att_w - pool_w)
    rows_p = lambda tile: _Rows(b_p, t_p, tile, mod_row0=b_s)
    rows_s = lambda tile: _Rows(b_s, t_s, tile, mod_row0=0)
    pad = (-(b_s + b_p)) % 8
    c_all = jnp.concatenate([c_sample, c_prompt, jnp.zeros((pad, d), F32)], axis=0)

    yp, ys = x_prompt, x_sample
    outs = [[] for _ in range(6)]
    for l in range(depth):
        mod = _modulation(c_all, w_cond[l], b_cond[l])
        mod3 = mod.reshape(mod.shape[0], 1, mod.shape[1])

        rs, rp = rows_s(TILE_INPROJ), rows_p(TILE_INPROJ)
        qs, ks, vs, us, gls, w_in_b = _inproj(rs, rs.view(ys), mod3, g_mix_pre[l], w_in[l], widths,
                                              tn=COLS_SHORT[0], emit=True)
        qp, kp, vp, up, glp = _inproj(rp, rp.view(yp), mod3, g_mix_pre[l], w_in_b, widths,
                                      tn=COLS_LONG[0], emit=False)

        a_p = _attn_prompt(qp, kp, vp, sb_bias[l], b_p, t_p, n_heads)
        a_s = _attn_sample(qs, ks, vs, cache_k[l], cache_v[l], page_table, sb_bias[l], b_s, t_s, n_heads)

        w_a, w_p, w_o = (w.astype(BF16) for w in (w_branch_att[l], w_branch_pool[l], w_out[l]))
        rs, rp = rows_s(TILE_MIXER), rows_p(TILE_MIXER)
        hist_s = jnp.pad(state_pool[l], ((0, 0), (POOL_HIST - POOL_BUF, 0), (0, 0)))
        ys = _mixer(rs, a_s, us, hist_s, lambda i: (0, 0, 0), gls, rs.view(ys), mod3, w_pool_group[l],
                    pool_scale[l], b_gate[l], w_a, w_p, w_o, g_mix_post[l], past_len=past_len, fresh=False)
        per_tile = rp.R // POOL_HIST
        yp = _mixer(rp, a_p, up, up.reshape(up.shape[0] // POOL_HIST, POOL_HIST, pool_w),
                    lambda i: (jnp.maximum(i * per_tile - 1, 0), 0, 0), glp, rp.view(yp), mod3, w_pool_group[l],
                    pool_scale[l], b_gate[l], w_a, w_p, w_o, g_mix_post[l], past_len=0, fresh=True)

        rs, rp = rows_s(TILE_FFN), rows_p(TILE_FFN)
        ys, wg_b, wu_b, wo_b = _ffn(rs, rs.view(ys), mod3, g_ffn_pre[l], g_ffn_post[l], w_ffn_in[l], w_ffn_in[l],
                                    w_ffn_out[l], wu_col0=d_ff, tf=COLS_SHORT[1], emit=True)
        yp, = _ffn(rp, rp.view(yp), mod3, g_ffn_pre[l], g_ffn_post[l], wg_b, wu_b, wo_b,
                   wu_col0=0, tf=COLS_LONG[1], emit=False)
        yp, ys = yp.reshape(b_p, t_p, d), ys.reshape(b_s, t_s, d)

        up3, us3 = up.reshape(b_p, t_p, pool_w), us.reshape(b_s, t_s, pool_w)
        new = (kp.reshape(b_p, t_p, n_heads, HEAD_DIM), vp.reshape(b_p, t_p, n_heads, HEAD_DIM),
               up3[:, t_p - POOL_BUF:, :],
               ks.reshape(b_s, t_s, n_heads, HEAD_DIM), vs.reshape(b_s, t_s, n_heads, HEAD_DIM),
               jnp.concatenate([state_pool[l], us3], axis=1)[:, -POOL_BUF:, :])
        for lst, val in zip(outs, new):
            lst.append(val)
    return (yp, ys) + tuple(jnp.stack(o) for o in outs)
```

```python
import functools

import jax
import jax.numpy as jnp
from jax import lax
from jax.experimental import pallas as pl
from jax.experimental.pallas import tpu as pltpu

F32 = jnp.float32
BF16 = jnp.bfloat16

HEAD_DIM = 128
LANES = 128
POOL_WINDOWS = (2, 4, 8, 16)
POOL_BUF = max(POOL_WINDOWS) - 1
POOL_HIST = POOL_BUF + 1
EPS = 1e-6
SB_SCALE = HEAD_DIM ** -0.5
LOG2E = 1.4426950408889634

VMEM_LIMIT_BYTES = 60 * 1024 * 1024
ATT_BLOCK = 256
PAGES_PER_STEP = 16
TILE_INPROJ = 1024
TILE_MIXER = 256
TILE_FFN = 1024
COLS_LONG = (1024, 512)
COLS_SHORT = (1024, 512)


def _cparams(semantics):
    return pltpu.CompilerParams(dimension_semantics=semantics, vmem_limit_bytes=VMEM_LIMIT_BYTES)


def _dot(a, b):
    return jnp.dot(a, b, preferred_element_type=F32)


def _dot_nt(a, b):
    return lax.dot_general(a, b, (((1,), (1,)), ((), ())), preferred_element_type=F32)


def _rms(x, g):
    return x * lax.rsqrt(jnp.mean(x * x, axis=-1, keepdims=True) + EPS) * g


def _mod_kernel(c_ref, w_ref, b_ref, o_ref):
    o_ref[...] = _dot(jax.nn.silu(c_ref[...]), w_ref[...]) + b_ref[...]


def _modulation(c_all, w_cond, b_cond, tn=1024):
    m, d = c_all.shape
    n = w_cond.shape[1]
    return pl.pallas_call(
        _mod_kernel,
        grid=(n // tn,),
        in_specs=[pl.BlockSpec((m, d), lambda j: (0, 0)),
                  pl.BlockSpec((d, tn), lambda j: (0, j)),
                  pl.BlockSpec((1, tn), lambda j: (0, j))],
        out_specs=pl.BlockSpec((m, tn), lambda j: (0, j)),
        out_shape=jax.ShapeDtypeStruct((m, n), F32),
        compiler_params=_cparams(("arbitrary",)),
        name="modulation",
    )(c_all, w_cond, b_cond.reshape(1, n))


class _Rows:
    def __init__(self, n_seq, seq_len, tile, mod_row0):
        if seq_len >= tile:
            assert seq_len % tile == 0
            self.G, self.R = 1, tile
            self.tiles_per_seq = seq_len // tile
            self.n_tiles = n_seq * self.tiles_per_seq
            self.mod_map = lambda i, k: (mod_row0 + i // self.tiles_per_seq, 0, k)
        else:
            assert seq_len % 8 == 0 and n_seq * seq_len <= tile
            self.G, self.R = n_seq, seq_len
            self.tiles_per_seq = 1
            self.n_tiles = 1
            mod_blk = mod_row0 // n_seq
            assert mod_blk * n_seq == mod_row0
            self.mod_map = lambda i, k: (mod_blk, 0, k)
        self.tm = self.G * self.R
        self.n_rows = n_seq * seq_len

    def x_spec(self, d, n_grid=2, **kw):
        if n_grid == 2:
            return pl.BlockSpec((self.G, self.R, d), lambda i, j: (i, 0, 0), **kw)
        return pl.BlockSpec((self.G, self.R, d), lambda i: (i, 0, 0), **kw)

    def mod_spec(self, d, k, n_grid=2):
        if n_grid == 2:
            return pl.BlockSpec((self.G, 1, d), lambda i, j: self.mod_map(i, k))
        return pl.BlockSpec((self.G, 1, d), lambda i: self.mod_map(i, k))

    def view(self, x):
        return x.reshape(self.n_tiles * self.G, self.R, x.shape[-1])


def _inproj_kernel(x_ref, sh_ref, sc_ref, g_ref, w_ref, *rest, starts, seg_out, n_out, emit):
    o_refs, h_ref = rest[:n_out], rest[-1]
    j = pl.program_id(1)

    @pl.when(j == 0)
    def _():
        h = _rms(x_ref[...], g_ref[...]) * (1.0 + sc_ref[...]) + sh_ref[...]
        h_ref[...] = h.reshape(h_ref.shape).astype(h_ref.dtype)

    if emit:
        rest[n_out][...] = w_ref[...].astype(BF16)
    for s, o in enumerate(seg_out):
        @pl.when((j >= starts[s]) & (j < starts[s + 1]))
        def _(o_ref=o_refs[o]):
            o_ref[...] = _dot(h_ref[...], w_ref[...].astype(BF16))


def _inproj(rows, x3, mod3, g_pre, w_in, segments, out_widths, tn, emit):
    d = x3.shape[-1]
    assert sum(s[0] for s in segments) == w_in.shape[1]
    assert all(s[0] % tn == 0 and s[2] % tn == 0 for s in segments)
    starts = [0]
    for width, _, _ in segments:
        starts.append(starts[-1] + width // tn)
    n_steps = starts[-1]

    def out_spec(o):
        active = {}
        for s, (width, out, off) in enumerate(segments):
            if out == o:
                for t in range(width // tn):
                    active[starts[s] + t] = off // tn + t
        held, table = active[min(active)], []
        for step in range(n_steps):
            held = active.get(step, held)
            table.append(held)

        def col(j):
            c = jnp.int32(table[-1])
            for step in reversed(range(n_steps - 1)):
                c = jnp.where(j <= step, table[step], c)
            return c

        return pl.BlockSpec((rows.tm, tn), lambda i, j: (i, col(j)))

    w_spec = pl.BlockSpec((d, tn), lambda i, j: (0, j))
    kern = functools.partial(_inproj_kernel, starts=tuple(starts), seg_out=tuple(s[1] for s in segments),
                             n_out=len(out_widths), emit=emit)
    return pl.pallas_call(
        kern,
        grid=(rows.n_tiles, n_steps),
        in_specs=[rows.x_spec(d, pipeline_mode=pl.Buffered(1)),
                  rows.mod_spec(d, 0), rows.mod_spec(d, 1),
                  pl.BlockSpec((1, d), lambda i, j: (0, 0)),
                  w_spec],
        out_specs=[out_spec(o) for o in range(len(out_widths))] + ([w_spec] if emit else []),
        out_shape=[jax.ShapeDtypeStruct((rows.n_rows, w), F32) for w in out_widths]
                  + ([jax.ShapeDtypeStruct(w_in.shape, BF16)] if emit else []),
        scratch_shapes=[pltpu.VMEM((rows.tm, d), BF16)],
        compiler_params=_cparams(("parallel", "arbitrary")),
        name="inproj",
    )(x3, mod3, mod3, g_pre.reshape(1, d), w_in)


def _sb_logs(z, mask):
    t = jnp.log(1.0 + jnp.exp2(jnp.abs(z) * -LOG2E))
    log_beta = jnp.minimum(z, 0.0) - t
    neg_stay = z - log_beta
    if mask is not None:
        neg_stay = jnp.where(mask, neg_stay, 0.0)
    return log_beta, neg_stay


def _sb_weights(log_beta, neg_after, mask):
    w = jnp.exp2((log_beta - neg_after) * LOG2E)
    return w if mask is None else jnp.where(mask, w, 0.0)


def _tri(k):
    r = lax.broadcasted_iota(jnp.int32, (k, k), 0)
    c = lax.broadcasted_iota(jnp.int32, (k, k), 1)
    return (r > c).astype(BF16)


def _attn_prompt_kernel(bias_ref, tri_ref, q_ref, k_ref, v_ref, o_ref, qs_ref, c_ref, acc_ref, *, n_heads):
    blk = q_ref.shape[0]
    qi = pl.program_id(1)
    tri = tri_ref[...]
    qs_ref[...] = q_ref[...] * SB_SCALE

    def visit(start, mask, first):
        heads = range(n_heads)
        cols = [slice(h * HEAD_DIM, (h + 1) * HEAD_DIM) for h in heads]
        z = [_dot_nt(qs_ref[:, cols[h]], k_ref[pl.ds(start, blk), cols[h]]) + bias_ref[h] for h in heads]
        logs = [_sb_logs(z[h], mask) for h in heads]
        after = [_dot(logs[h][1].astype(BF16), tri) for h in heads]
        for h in heads:
            total = jnp.sum(logs[h][1], axis=-1, keepdims=True)
            if first:
                c_ref[h] = total
            else:
                after[h] = after[h] + c_ref[h]
                c_ref[h] += total
        w = [_sb_weights(logs[h][0], after[h], mask) for h in heads]
        for h in heads:
            pv = _dot(w[h], v_ref[pl.ds(start, blk), cols[h]])
            if first:
                acc_ref[h] = pv
            else:
                acc_ref[h] += pv

    r = lax.broadcasted_iota(jnp.int32, (blk, blk), 0)
    s = lax.broadcasted_iota(jnp.int32, (blk, blk), 1)
    visit(pl.multiple_of(qi * blk, blk), s < r, True)

    def body(it, _):
        visit(pl.multiple_of((qi - 1 - it) * blk, blk), None, False)
        return 0

    lax.fori_loop(0, qi, body, 0)
    for h in range(n_heads):
        o_ref[:, h * HEAD_DIM:(h + 1) * HEAD_DIM] = acc_ref[h]


def _attn_prompt(q, q_blk, k, v, sb_bias, n_seq, seq_len, n_heads, blk=ATT_BLOCK):
    nq = seq_len // blk
    att_w = n_heads * HEAD_DIM
    return pl.pallas_call(
        functools.partial(_attn_prompt_kernel, n_heads=n_heads),
        grid=(n_seq, nq),
        in_specs=[pl.BlockSpec(memory_space=pltpu.SMEM),
                  pl.BlockSpec((blk, blk), lambda b, i: (0, 0)),
                  pl.BlockSpec((blk, att_w), lambda b, i: (b * nq + i, q_blk)),
                  pl.BlockSpec((seq_len, att_w), lambda b, i: (b, 0), pipeline_mode=pl.Buffered(1)),
                  pl.BlockSpec((seq_len, att_w), lambda b, i: (b, 0), pipeline_mode=pl.Buffered(1))],
        out_specs=pl.BlockSpec((blk, att_w), lambda b, i: (b * nq + i, 0)),
        out_shape=jax.ShapeDtypeStruct((n_seq * seq_len, att_w), F32),
        scratch_shapes=[pltpu.VMEM((blk, att_w), F32),
                        pltpu.VMEM((n_heads, blk, 1), F32),
                        pltpu.VMEM((n_heads, blk, HEAD_DIM), F32)],
        compiler_params=_cparams(("parallel", "arbitrary")),
        name="attn_prompt",
    )(sb_bias, _tri(blk), q, k, v)


def _attn_sample_kernel(pt_ref, tri2_ref, bias_ref, q_ref, kn_ref, vn_ref, *rest, n_heads, t_new, pages):
    k_refs, v_refs = rest[:pages], rest[pages:2 * pages]
    o_ref, q2_ref, c_ref, acc_ref = rest[2 * pages:]
    s = pl.program_id(1)
    tri2 = tri2_ref[...]

    def head_masks(width):
        lane_head = lax.broadcasted_iota(jnp.int32, (t_new, width), 1) % n_heads
        return [lane_head == h for h in range(n_heads)]

    def lanes(x, j):
        return x[:, j * LANES:(j + 1) * LANES]

    def visit(kmats, vmats, bias, mask):
        hm = head_masks(kmats[0].shape[0])
        logs = []
        for km in kmats:
            zf = _dot_nt(q2_ref[...], km)
            z = zf[0:t_new]
            for h in range(1, n_heads):
                z = jnp.where(hm[h], zf[h * t_new:(h + 1) * t_new], z)
            logs.append(_sb_logs(z + bias, mask))
        n_ch = kmats[0].shape[0] // LANES
        order = [(i, j) for i in range(len(kmats)) for j in reversed(range(n_ch))]
        lhs = jnp.concatenate([lanes(logs[i][1], j) for i, j in order], axis=0)
        hi = lhs.astype(BF16)
        lo = (lhs - hi.astype(F32)).astype(BF16)
        st = _dot(hi, tri2) + _dot(lo, tri2)
        run = c_ref[...]
        after = {}
        for idx, ij in enumerate(order):
            blk = st[idx * t_new:(idx + 1) * t_new]
            after[ij] = blk[:, :LANES] + run
            run = run + blk[:, LANES:]
        c_ref[...] = run
        pv = None
        for i, vm in enumerate(vmats):
            aft = jnp.concatenate([after[(i, j)] for j in range(n_ch)], axis=1)
            w = _sb_weights(logs[i][0], aft, mask)
            we = jnp.concatenate([jnp.where(hm[h], w, 0.0) for h in range(n_heads)], axis=0)
            d = _dot(we, vm)
            pv = d if pv is None else pv + d
        acc_ref[...] += pv

    @pl.when(s == 0)
    def _():
        for h in range(n_heads):
            q2_ref[h * t_new:(h + 1) * t_new, :] = q_ref[:, h * HEAD_DIM:(h + 1) * HEAD_DIM] * SB_SCALE
        c_ref[...] = jnp.zeros_like(c_ref)
        acc_ref[...] = jnp.zeros_like(acc_ref)
        pad = jnp.zeros((LANES - kn_ref.shape[0], HEAD_DIM), F32)
        k_pos = lax.broadcasted_iota(jnp.int32, (t_new, LANES), 1) // n_heads
        q_pos = lax.broadcasted_iota(jnp.int32, (t_new, LANES), 0)
        visit([jnp.concatenate([kn_ref[...], pad], axis=0)], [jnp.concatenate([vn_ref[...], pad], axis=0)],
              bias_ref[:, :LANES], k_pos < q_pos)

    visit([r[...] for r in k_refs], [r[...] for r in v_refs], bias_ref[...], None)

    @pl.when(s == pl.num_programs(1) - 1)
    def _():
        for h in range(n_heads):
            o_ref[:, h * HEAD_DIM:(h + 1) * HEAD_DIM] = acc_ref[h * t_new:(h + 1) * t_new, :]


def _attn_sample(q, q_blk, k, v, cache_k, cache_v, page_table, sb_bias, n_seq, t_new, n_heads,
                 pages=PAGES_PER_STEP):
    n_phys, page = cache_k.shape[0], cache_k.shape[1]
    n_pages = page_table.shape[1]
    assert n_pages % pages == 0
    att_w = n_heads * HEAD_DIM
    rows = page * n_heads
    new_rows = t_new * n_heads
    assert LANES % n_heads == 0 and rows % LANES == 0 and new_rows <= LANES and t_new % 8 == 0
    ck = cache_k.reshape(n_phys, rows, HEAD_DIM)
    cv = cache_v.reshape(n_phys, rows, HEAD_DIM)
    kn = k.reshape(n_seq, new_rows, HEAD_DIM)
    vn = v.reshape(n_seq, new_rows, HEAD_DIM)
    bias = jnp.broadcast_to(jnp.tile(sb_bias, rows // n_heads)[None, :], (t_new, rows))
    r = lax.broadcasted_iota(jnp.int32, (LANES, LANES), 0)
    c = lax.broadcasted_iota(jnp.int32, (LANES, LANES), 1)
    same_head = (r % n_heads) == (c % n_heads)
    tri2 = jnp.concatenate([same_head & (r // n_heads > c // n_heads), same_head], axis=1).astype(BF16)

    def page_spec(i):
        return pl.BlockSpec((None, rows, HEAD_DIM),
                            lambda b, s, pt: (pt[b, n_pages - 1 - (s * pages + i)], 0, 0))

    grid_spec = pltpu.PrefetchScalarGridSpec(
        num_scalar_prefetch=1,
        grid=(n_seq, n_pages // pages),
        in_specs=[pl.BlockSpec((LANES, 2 * LANES), lambda b, s, pt: (0, 0)),
                  pl.BlockSpec((t_new, rows), lambda b, s, pt: (0, 0)),
                  pl.BlockSpec((t_new, att_w), lambda b, s, pt: (b, q_blk)),
                  pl.BlockSpec((None, new_rows, HEAD_DIM), lambda b, s, pt: (b, 0, 0)),
                  pl.BlockSpec((None, new_rows, HEAD_DIM), lambda b, s, pt: (b, 0, 0))]
                 + [page_spec(i) for i in range(pages)] * 2,
        out_specs=pl.BlockSpec((t_new, att_w), lambda b, s, pt: (b, 0)),
        scratch_shapes=[pltpu.VMEM((n_heads * t_new, HEAD_DIM), F32),
                        pltpu.VMEM((t_new, LANES), F32),
                        pltpu.VMEM((n_heads * t_new, HEAD_DIM), F32)],
    )
    kern = functools.partial(_attn_sample_kernel, n_heads=n_heads, t_new=t_new, pages=pages)
    return pl.pallas_call(
        kern,
        grid_spec=grid_spec,
        out_shape=jax.ShapeDtypeStruct((n_seq * t_new, att_w), F32),
        compiler_params=_cparams(("parallel", "arbitrary")),
        name="attn_sample",
    )(page_table, tri2, bias, q, kn, vn, *([ck] * pages), *([cv] * pages))


def _pool_mix(u_ref, hist_ref, wg_ref, ps_ref, p_ref, ext_ref, pa_ref, pb_ref, *, G, R, t0, past_len, fresh):
    c_all = u_ref.shape[-1]
    cg = c_all // len(POOL_WINDOWS)
    n = POOL_HIST + R
    cur = u_ref[...].reshape(G, R, c_all)
    hist = hist_ref[...]
    if fresh:
        hist = jnp.where(t0 == 0, 0.0, hist)
    ext_ref[:, 0:POOL_HIST, :] = hist
    ext_ref[:, POOL_HIST:n, :] = cur

    src, dst = ext_ref, pa_ref
    for lvl, win in enumerate(POOL_WINDOWS):
        half = win // 2
        lo = lvl * cg
        rows_out = n - (win - 1)
        dst[:, 0:rows_out, lo:] = src[:, half:half + rows_out, lo:] + src[:, 0:rows_out, lo:]
        src, dst = dst, (pb_ref if dst is pa_ref else pa_ref)

    pos = past_len + t0 + lax.broadcasted_iota(jnp.int32, (1, R, cg), 1)
    for g, win in enumerate(POOL_WINDOWS):
        buf = pa_ref if g % 2 == 0 else pb_ref
        cols = slice(g * cg, (g + 1) * cg)
        first = POOL_HIST - (win - 1)
        wsum = buf[:, first:first + R, cols]
        cnt = jnp.minimum(win, pos + 1).astype(F32)
        pooled = wsum / cnt - cur[:, :, cols]
        mixed = _dot(pooled.reshape(G * R, cg), wg_ref[g])
        p_ref[:, cols] = (mixed * ps_ref[:, cols]).astype(p_ref.dtype)


def _mixer_kernel(a_ref, u_ref, hist_ref, gl_ref, x_ref, gt_ref, wg_ref, ps_ref, bg_ref, wa_ref, wp_ref, wo_ref,
                  gpost_ref, o_ref, p_ref, ext_ref, pa_ref, pb_ref, *, G, R, tiles_per_seq, past_len, fresh):
    d = x_ref.shape[-1]
    t0 = (pl.program_id(0) % tiles_per_seq) * R if fresh else 0
    merged_a = jax.nn.sigmoid(gl_ref[:, :d] + bg_ref[:, :d]) * _dot(a_ref[...].astype(BF16), wa_ref[...])
    _pool_mix(u_ref, hist_ref, wg_ref, ps_ref, p_ref, ext_ref, pa_ref, pb_ref,
              G=G, R=R, t0=t0, past_len=past_len, fresh=fresh)
    merged = merged_a + jax.nn.sigmoid(gl_ref[:, d:] + bg_ref[:, d:]) * _dot(p_ref[...], wp_ref[...])
    y = _dot(merged.astype(BF16), wo_ref[...])
    o_ref[...] = x_ref[...] + gt_ref[...] * _rms(y, gpost_ref[...]).reshape(x_ref.shape)


def _mixer(rows, a, glu, u_blk, hist3, hist_map, x3, mod3, w_pool_group, pool_scale, b_gate, w_a, w_p, w_o, g_post,
           past_len, fresh):
    d = x3.shape[-1]
    n_groups, cg = w_pool_group.shape[0], w_pool_group.shape[1]
    pool_w, att_w = pool_scale.shape[0], a.shape[1]
    G, R, tm = rows.G, rows.R, rows.tm
    const = lambda shape: pl.BlockSpec(shape, lambda i: (0,) * len(shape), pipeline_mode=pl.Buffered(1))
    ext = pltpu.VMEM((G, POOL_HIST + R, pool_w), F32)
    kern = functools.partial(_mixer_kernel, G=G, R=R, tiles_per_seq=rows.tiles_per_seq,
                             past_len=past_len, fresh=fresh)
    return pl.pallas_call(
        kern,
        grid=(rows.n_tiles,),
        in_specs=[pl.BlockSpec((tm, att_w), lambda i: (i, 0)),
                  pl.BlockSpec((tm, pool_w), lambda i: (i, u_blk)),
                  pl.BlockSpec((G, POOL_HIST, pool_w), hist_map),
                  pl.BlockSpec((tm, 2 * d), lambda i: (i, 0)),
                  rows.x_spec(d, n_grid=1),
                  rows.mod_spec(d, 2, n_grid=1),
                  const((n_groups, cg, cg)), const((1, pool_w)), const((1, 2 * d)),
                  const((att_w, d)), const((pool_w, d)), const((d, d)), const((1, d))],
        out_specs=rows.x_spec(d, n_grid=1),
        out_shape=jax.ShapeDtypeStruct(x3.shape, F32),
        scratch_shapes=[pltpu.VMEM((tm, pool_w), BF16), ext, ext, ext],
        compiler_params=_cparams(("parallel",)),
        name="mixer",
    )(a, glu, hist3, glu, x3, mod3, w_pool_group, pool_scale.reshape(1, pool_w), b_gate.reshape(1, 2 * d),
      w_a, w_p, w_o, g_post.reshape(1, d))


def _ffn_kernel(x_ref, sh_ref, sc_ref, gt_ref, gpre_ref, gpost_ref, wg_ref, wu_ref, wo_ref, o_ref, *rest, emit):
    h_ref = rest[-1]
    j = pl.program_id(1)

    @pl.when(j == 0)
    def _():
        h = _rms(x_ref[...], gpre_ref[...]) * (1.0 + sc_ref[...]) + sh_ref[...]
        h_ref[...] = h.reshape(h_ref.shape).astype(h_ref.dtype)
        o_ref[...] = jnp.zeros_like(o_ref)

    wg, wu, wo = (r[...].astype(BF16) for r in (wg_ref, wu_ref, wo_ref))
    if emit:
        for ref, val in zip(rest[:3], (wg, wu, wo)):
            ref[...] = val
    h = h_ref[...]
    act = jax.nn.silu(_dot(h, wg)) * _dot(h, wu)
    o_ref[...] += _dot(act.astype(BF16), wo).reshape(o_ref.shape)

    @pl.when(j == pl.num_programs(1) - 1)
    def _():
        o_ref[...] = x_ref[...] + gt_ref[...] * _rms(o_ref[...], gpost_ref[...])


def _ffn(rows, x3, mod3, g_pre, g_post, wg, wu, wo, wu_col0, tf, emit):
    d = x3.shape[-1]
    d_ff = wo.shape[0]
    nf = d_ff // tf
    assert nf * tf == d_ff and wu_col0 % tf == 0
    u0 = wu_col0 // tf
    w_specs = [pl.BlockSpec((d, tf), lambda i, j: (0, j)),
               pl.BlockSpec((d, tf), lambda i, j: (0, u0 + j)),
               pl.BlockSpec((tf, d), lambda i, j: (j, 0))]
    emit_specs = [pl.BlockSpec((d, tf), lambda i, j: (0, j)),
                  pl.BlockSpec((d, tf), lambda i, j: (0, j)),
                  pl.BlockSpec((tf, d), lambda i, j: (j, 0))]
    emit_shapes = [jax.ShapeDtypeStruct((d, d_ff), BF16), jax.ShapeDtypeStruct((d, d_ff), BF16),
                   jax.ShapeDtypeStruct((d_ff, d), BF16)]
    return pl.pallas_call(
        functools.partial(_ffn_kernel, emit=emit),
        grid=(rows.n_tiles, nf),
        in_specs=[rows.x_spec(d, pipeline_mode=pl.Buffered(1)),
                  rows.mod_spec(d, 3), rows.mod_spec(d, 4), rows.mod_spec(d, 5),
                  pl.BlockSpec((1, d), lambda i, j: (0, 0)),
                  pl.BlockSpec((1, d), lambda i, j: (0, 0))] + w_specs,
        out_specs=[rows.x_spec(d)] + (emit_specs if emit else []),
        out_shape=[jax.ShapeDtypeStruct(x3.shape, F32)] + (emit_shapes if emit else []),
        scratch_shapes=[pltpu.VMEM((rows.tm, d), BF16)],
        compiler_params=_cparams(("parallel", "arbitrary")),
        name="ffn",
    )(x3, mod3, mod3, mod3, g_pre.reshape(1, d), g_post.reshape(1, d), wg, wu, wo)


def kernel(x_prompt, x_sample, c_prompt, c_sample, cache_k, cache_v, page_table, state_pool, w_cond, b_cond, g_mix_pre, g_mix_post, g_ffn_pre, g_ffn_post, w_in, b_gate, sb_bias, w_pool_group, pool_scale, w_branch_att, w_branch_pool, w_out, w_ffn_in, w_ffn_out):
    depth = w_cond.shape[0]
    b_p, t_p, d = x_prompt.shape
    b_s, t_s, _ = x_sample.shape
    n_heads = sb_bias.shape[1]
    att_w = n_heads * HEAD_DIM
    pool_w = pool_scale.shape[1]
    d_ff = w_ffn_out.shape[1]
    past_len = page_table.shape[1] * cache_k.shape[2]
    gl_w = w_in.shape[2] - 3 * att_w - pool_w
    assert gl_w % att_w == 0 and (gl_w + att_w) % pool_w == 0
    segments = ((att_w, 0, gl_w), (att_w, 1, 0), (att_w, 2, 0), (pool_w, 0, gl_w + att_w), (gl_w, 0, 0))
    out_widths = (gl_w + att_w + pool_w, att_w, att_w)
    q_blk, u_blk, u_col = gl_w // att_w, (gl_w + att_w) // pool_w, gl_w + att_w
    rows_p = lambda tile: _Rows(b_p, t_p, tile, mod_row0=b_s)
    rows_s = lambda tile: _Rows(b_s, t_s, tile, mod_row0=0)
    pad = (-(b_s + b_p)) % 8
    c_all = jnp.concatenate([c_sample, c_prompt, jnp.zeros((pad, d), F32)], axis=0)

    yp, ys = x_prompt, x_sample
    outs = [[] for _ in range(6)]
    for l in range(depth):
        mod = _modulation(c_all, w_cond[l], b_cond[l])
        mod3 = mod.reshape(mod.shape[0], 1, mod.shape[1])

        rs, rp = rows_s(TILE_INPROJ), rows_p(TILE_INPROJ)
        glu_s, ks, vs, w_in_b = _inproj(rs, rs.view(ys), mod3, g_mix_pre[l], w_in[l], segments, out_widths,
                                        tn=COLS_SHORT[0], emit=True)
        glu_p, kp, vp = _inproj(rp, rp.view(yp), mod3, g_mix_pre[l], w_in_b, segments, out_widths,
                                tn=COLS_LONG[0], emit=False)

        a_p = _attn_prompt(glu_p, q_blk, kp, vp, sb_bias[l], b_p, t_p, n_heads)
        a_s = _attn_sample(glu_s, q_blk, ks, vs, cache_k[l], cache_v[l], page_table, sb_bias[l], b_s, t_s, n_heads)

        w_a, w_p, w_o = (w.astype(BF16) for w in (w_branch_att[l], w_branch_pool[l], w_out[l]))
        rs, rp = rows_s(TILE_MIXER), rows_p(TILE_MIXER)
        hist_s = jnp.pad(state_pool[l], ((0, 0), (POOL_HIST - POOL_BUF, 0), (0, 0)))
        ys = _mixer(rs, a_s, glu_s, u_blk, hist_s, lambda i: (0, 0, 0), rs.view(ys), mod3, w_pool_group[l],
                    pool_scale[l], b_gate[l], w_a, w_p, w_o, g_mix_post[l], past_len=past_len, fresh=False)
        per_tile = rp.R // POOL_HIST
        yp = _mixer(rp, a_p, glu_p, u_blk, glu_p.reshape(glu_p.shape[0] // POOL_HIST, POOL_HIST, glu_p.shape[1]),
                    lambda i: (jnp.maximum(i * per_tile - 1, 0), 0, u_blk), rp.view(yp), mod3, w_pool_group[l],
                    pool_scale[l], b_gate[l], w_a, w_p, w_o, g_mix_post[l], past_len=0, fresh=True)

        rs, rp = rows_s(TILE_FFN), rows_p(TILE_FFN)
        ys, wg_b, wu_b, wo_b = _ffn(rs, rs.view(ys), mod3, g_ffn_pre[l], g_ffn_post[l], w_ffn_in[l], w_ffn_in[l],
                                    w_ffn_out[l], wu_col0=d_ff, tf=COLS_SHORT[1], emit=True)
        yp, = _ffn(rp, rp.view(yp), mod3, g_ffn_pre[l], g_ffn_post[l], wg_b, wu_b, wo_b,
                   wu_col0=0, tf=COLS_LONG[1], emit=False)
        yp, ys = yp.reshape(b_p, t_p, d), ys.reshape(b_s, t_s, d)

        up3 = glu_p.reshape(b_p, t_p, -1)[:, t_p - POOL_BUF:, u_col:u_col + pool_w]
        us3 = glu_s.reshape(b_s, t_s, -1)[:, :, u_col:u_col + pool_w]
        new = (kp.reshape(b_p, t_p, n_heads, HEAD_DIM), vp.reshape(b_p, t_p, n_heads, HEAD_DIM),
               up3,
               ks.reshape(b_s, t_s, n_heads, HEAD_DIM), vs.reshape(b_s, t_s, n_heads, HEAD_DIM),
               jnp.concatenate([state_pool[l], us3], axis=1)[:, -POOL_BUF:, :])
        for lst, val in zip(outs, new):
            lst.append(val)
    return (yp, ys) + tuple(jnp.stack(o) for o in outs)
```

```python
import functools

import jax
import jax.numpy as jnp
from jax import lax
from jax.experimental import pallas as pl
from jax.experimental.pallas import tpu as pltpu

F32 = jnp.float32
BF16 = jnp.bfloat16

HEAD_DIM = 128
LANES = 128
POOL_WINDOWS = (2, 4, 8, 16)
POOL_BUF = max(POOL_WINDOWS) - 1
POOL_HIST = POOL_BUF + 1
EPS = 1e-6
SB_SCALE = HEAD_DIM ** -0.5
LOG2E = 1.4426950408889634

VMEM_LIMIT_BYTES = 60 * 1024 * 1024
ATT_BLOCK = 256
PAGES_PER_STEP = 16
TILE_INPROJ = 1024
TILE_MIXER = 256
TILE_FFN = 1024
ROW_CHUNK = 256
COLS_LONG = (1024, 512)
COLS_SHORT = (1024, 512)


def _cparams(semantics):
    return pltpu.CompilerParams(dimension_semantics=semantics, vmem_limit_bytes=VMEM_LIMIT_BYTES)


def _dot(a, b):
    return jnp.dot(a, b, preferred_element_type=F32)


def _dot_nt(a, b):
    return lax.dot_general(a, b, (((1,), (1,)), ((), ())), preferred_element_type=F32)


def _rms(x, g):
    return x * lax.rsqrt(jnp.mean(x * x, axis=-1, keepdims=True) + EPS) * g


def _row_chunks(block_shape):
    g, r, _ = block_shape
    if g > 1 or r <= ROW_CHUNK:
        return [slice(None)]
    assert r % ROW_CHUNK == 0
    return [slice(c * ROW_CHUNK, (c + 1) * ROW_CHUNK) for c in range(r // ROW_CHUNK)]


def _mod_kernel(c_ref, w_ref, b_ref, o_ref):
    o_ref[...] = _dot(jax.nn.silu(c_ref[...]), w_ref[...]) + b_ref[...]


def _modulation(c_all, w_cond, b_cond, tn=1024):
    m, d = c_all.shape
    n = w_cond.shape[1]
    return pl.pallas_call(
        _mod_kernel,
        grid=(n // tn,),
        in_specs=[pl.BlockSpec((m, d), lambda j: (0, 0)),
                  pl.BlockSpec((d, tn), lambda j: (0, j)),
                  pl.BlockSpec((1, tn), lambda j: (0, j))],
        out_specs=pl.BlockSpec((m, tn), lambda j: (0, j)),
        out_shape=jax.ShapeDtypeStruct((m, n), F32),
        compiler_params=_cparams(("arbitrary",)),
        name="modulation",
    )(c_all, w_cond, b_cond.reshape(1, n))


class _Rows:
    def __init__(self, n_seq, seq_len, tile, mod_row0):
        if seq_len >= tile:
            assert seq_len % tile == 0
            self.G, self.R = 1, tile
            self.tiles_per_seq = seq_len // tile
            self.n_tiles = n_seq * self.tiles_per_seq
            self.mod_map = lambda i, k: (mod_row0 + i // self.tiles_per_seq, 0, k)
        else:
            assert seq_len % 8 == 0 and n_seq * seq_len <= tile
            self.G, self.R = n_seq, seq_len
            self.tiles_per_seq = 1
            self.n_tiles = 1
            mod_blk = mod_row0 // n_seq
            assert mod_blk * n_seq == mod_row0
            self.mod_map = lambda i, k: (mod_blk, 0, k)
        self.tm = self.G * self.R
        self.n_rows = n_seq * seq_len

    def x_spec(self, d, n_grid=2, **kw):
        if n_grid == 2:
            return pl.BlockSpec((self.G, self.R, d), lambda i, j: (i, 0, 0), **kw)
        return pl.BlockSpec((self.G, self.R, d), lambda i: (i, 0, 0), **kw)

    def mod_spec(self, d, k, n_grid=2):
        if n_grid == 2:
            return pl.BlockSpec((self.G, 1, d), lambda i, j: self.mod_map(i, k))
        return pl.BlockSpec((self.G, 1, d), lambda i: self.mod_map(i, k))

    def view(self, x):
        return x.reshape(self.n_tiles * self.G, self.R, x.shape[-1])


def _inproj_kernel(x_ref, sh_ref, sc_ref, g_ref, w_ref, *rest, starts, seg_out, n_out, emit):
    o_refs, h_ref = rest[:n_out], rest[-1]
    j = pl.program_id(1)
    w = w_ref[...].astype(BF16)
    if emit:
        rest[n_out][...] = w

    @pl.when(j == 0)
    def _():
        for rows in _row_chunks(x_ref.shape):
            h = _rms(x_ref[:, rows, :], g_ref[...]) * (1.0 + sc_ref[...]) + sh_ref[...]
            h = h.reshape(-1, h.shape[-1]).astype(h_ref.dtype)
            flat = rows if x_ref.shape[0] == 1 else slice(None)
            h_ref[flat, :] = h
            o_refs[seg_out[0]][flat, :] = _dot(h, w)

    for s, o in enumerate(seg_out):
        @pl.when((j >= max(starts[s], 1)) & (j < starts[s + 1]))
        def _(o_ref=o_refs[o]):
            o_ref[...] = _dot(h_ref[...], w)


def _inproj(rows, x3, mod3, g_pre, w_in, segments, out_widths, tn, emit):
    d = x3.shape[-1]
    assert sum(s[0] for s in segments) == w_in.shape[1]
    assert all(s[0] % tn == 0 and s[2] % tn == 0 for s in segments)
    starts = [0]
    for width, _, _ in segments:
        starts.append(starts[-1] + width // tn)
    n_steps = starts[-1]

    def out_spec(o):
        active = {}
        for s, (width, out, off) in enumerate(segments):
            if out == o:
                for t in range(width // tn):
                    active[starts[s] + t] = off // tn + t
        held, table = active[min(active)], []
        for step in range(n_steps):
            held = active.get(step, held)
            table.append(held)

        def col(j):
            c = jnp.int32(table[-1])
            for step in reversed(range(n_steps - 1)):
                c = jnp.where(j <= step, table[step], c)
            return c

        return pl.BlockSpec((rows.tm, tn), lambda i, j: (i, col(j)))

    w_spec = pl.BlockSpec((d, tn), lambda i, j: (0, j))
    kern = functools.partial(_inproj_kernel, starts=tuple(starts), seg_out=tuple(s[1] for s in segments),
                             n_out=len(out_widths), emit=emit)
    return pl.pallas_call(
        kern,
        grid=(rows.n_tiles, n_steps),
        in_specs=[rows.x_spec(d, pipeline_mode=pl.Buffered(1)),
                  rows.mod_spec(d, 0), rows.mod_spec(d, 1),
                  pl.BlockSpec((1, d), lambda i, j: (0, 0)),
                  w_spec],
        out_specs=[out_spec(o) for o in range(len(out_widths))] + ([w_spec] if emit else []),
        out_shape=[jax.ShapeDtypeStruct((rows.n_rows, w), F32) for w in out_widths]
                  + ([jax.ShapeDtypeStruct(w_in.shape, BF16)] if emit else []),
        scratch_shapes=[pltpu.VMEM((rows.tm, d), BF16)],
        compiler_params=_cparams(("parallel", "arbitrary")),
        name="inproj",
    )(x3, mod3, mod3, g_pre.reshape(1, d), w_in)


def _sb_logs(z, mask):
    t = jnp.log(1.0 + jnp.exp2(jnp.abs(z) * -LOG2E))
    log_beta = jnp.minimum(z, 0.0) - t
    neg_stay = z - log_beta
    if mask is not None:
        neg_stay = jnp.where(mask, neg_stay, 0.0)
    return log_beta, neg_stay


def _sb_weights(log_beta, neg_after, mask):
    w = jnp.exp2((log_beta - neg_after) * LOG2E)
    return w if mask is None else jnp.where(mask, w, 0.0)


def _tri(k):
    r = lax.broadcasted_iota(jnp.int32, (k, k), 0)
    c = lax.broadcasted_iota(jnp.int32, (k, k), 1)
    return (r > c).astype(BF16)


def _attn_prompt_kernel(bias_ref, tri_ref, q_ref, k_ref, v_ref, o_ref, qs_ref, c_ref, acc_ref, *, n_heads):
    blk = q_ref.shape[0]
    qi = pl.program_id(1)
    tri = tri_ref[...]
    qs_ref[...] = q_ref[...] * SB_SCALE

    def visit(start, mask, first):
        heads = range(n_heads)
        cols = [slice(h * HEAD_DIM, (h + 1) * HEAD_DIM) for h in heads]
        z = [_dot_nt(qs_ref[:, cols[h]], k_ref[pl.ds(start, blk), cols[h]]) + bias_ref[h] for h in heads]
        logs = [_sb_logs(z[h], mask) for h in heads]
        after = [_dot(logs[h][1].astype(BF16), tri) for h in heads]
        for h in heads:
            total = jnp.sum(logs[h][1], axis=-1, keepdims=True)
            if first:
                c_ref[h] = total
            else:
                after[h] = after[h] + c_ref[h]
                c_ref[h] += total
        w = [_sb_weights(logs[h][0], after[h], mask) for h in heads]
        for h in heads:
            pv = _dot(w[h], v_ref[pl.ds(start, blk), cols[h]])
            if first:
                acc_ref[h] = pv
            else:
                acc_ref[h] += pv

    r = lax.broadcasted_iota(jnp.int32, (blk, blk), 0)
    s = lax.broadcasted_iota(jnp.int32, (blk, blk), 1)
    visit(pl.multiple_of(qi * blk, blk), s < r, True)

    def body(it, _):
        visit(pl.multiple_of((qi - 1 - it) * blk, blk), None, False)
        return 0

    lax.fori_loop(0, qi, body, 0)
    for h in range(n_heads):
        o_ref[:, h * HEAD_DIM:(h + 1) * HEAD_DIM] = acc_ref[h]


def _attn_prompt(q, q_blk, k, v, sb_bias, n_seq, seq_len, n_heads, blk=ATT_BLOCK):
    nq = seq_len // blk
    att_w = n_heads * HEAD_DIM
    return pl.pallas_call(
        functools.partial(_attn_prompt_kernel, n_heads=n_heads),
        grid=(n_seq, nq),
        in_specs=[pl.BlockSpec(memory_space=pltpu.SMEM),
                  pl.BlockSpec((blk, blk), lambda b, i: (0, 0)),
                  pl.BlockSpec((blk, att_w), lambda b, i: (b * nq + i, q_blk)),
                  pl.BlockSpec((seq_len, att_w), lambda b, i: (b, 0), pipeline_mode=pl.Buffered(1)),
                  pl.BlockSpec((seq_len, att_w), lambda b, i: (b, 0), pipeline_mode=pl.Buffered(1))],
        out_specs=pl.BlockSpec((blk, att_w), lambda b, i: (b * nq + i, 0)),
        out_shape=jax.ShapeDtypeStruct((n_seq * seq_len, att_w), F32),
        scratch_shapes=[pltpu.VMEM((blk, att_w), F32),
                        pltpu.VMEM((n_heads, blk, 1), F32),
                        pltpu.VMEM((n_heads, blk, HEAD_DIM), F32)],
        compiler_params=_cparams(("parallel", "arbitrary")),
        name="attn_prompt",
    )(sb_bias, _tri(blk), q, k, v)


def _attn_sample_kernel(pt_ref, tri2_ref, bias_ref, q_ref, kn_ref, vn_ref, *rest, n_heads, t_new, pages):
    k_refs, v_refs = rest[:pages], rest[pages:2 * pages]
    o_ref, q2_ref, c_ref, acc_ref = rest[2 * pages:]
    s = pl.program_id(1)
    tri2 = tri2_ref[...]

    def head_masks(width):
        lane_head = lax.broadcasted_iota(jnp.int32, (t_new, width), 1) % n_heads
        return [lane_head == h for h in range(n_heads)]

    def lanes(x, j):
        return x[:, j * LANES:(j + 1) * LANES]

    def visit(kmats, vmats, bias, mask):
        hm = head_masks(kmats[0].shape[0])
        logs = []
        for km in kmats:
            zf = _dot_nt(q2_ref[...], km)
            z = zf[0:t_new]
            for h in range(1, n_heads):
                z = jnp.where(hm[h], zf[h * t_new:(h + 1) * t_new], z)
            logs.append(_sb_logs(z + bias, mask))
        n_ch = kmats[0].shape[0] // LANES
        order = [(i, j) for i in range(len(kmats)) for j in reversed(range(n_ch))]
        lhs = jnp.concatenate([lanes(logs[i][1], j) for i, j in order], axis=0)
        hi = lhs.astype(BF16)
        lo = (lhs - hi.astype(F32)).astype(BF16)
        st = _dot(hi, tri2) + _dot(lo, tri2)
        run = c_ref[...]
        after = {}
        for idx, ij in enumerate(order):
            blk = st[idx * t_new:(idx + 1) * t_new]
            after[ij] = blk[:, :LANES] + run
            run = run + blk[:, LANES:]
        c_ref[...] = run
        pv = None
        for i, vm in enumerate(vmats):
            aft = jnp.concatenate([after[(i, j)] for j in range(n_ch)], axis=1)
            w = _sb_weights(logs[i][0], aft, mask)
            we = jnp.concatenate([jnp.where(hm[h], w, 0.0) for h in range(n_heads)], axis=0)
            d = _dot(we, vm)
            pv = d if pv is None else pv + d
        acc_ref[...] += pv

    @pl.when(s == 0)
    def _():
        for h in range(n_heads):
            q2_ref[h * t_new:(h + 1) * t_new, :] = q_ref[:, h * HEAD_DIM:(h + 1) * HEAD_DIM] * SB_SCALE
        c_ref[...] = jnp.zeros_like(c_ref)
        acc_ref[...] = jnp.zeros_like(acc_ref)
        pad = jnp.zeros((LANES - kn_ref.shape[0], HEAD_DIM), F32)
        k_pos = lax.broadcasted_iota(jnp.int32, (t_new, LANES), 1) // n_heads
        q_pos = lax.broadcasted_iota(jnp.int32, (t_new, LANES), 0)
        visit([jnp.concatenate([kn_ref[...], pad], axis=0)], [jnp.concatenate([vn_ref[...], pad], axis=0)],
              bias_ref[:, :LANES], k_pos < q_pos)

    visit([r[...] for r in k_refs], [r[...] for r in v_refs], bias_ref[...], None)

    @pl.when(s == pl.num_programs(1) - 1)
    def _():
        for h in range(n_heads):
            o_ref[:, h * HEAD_DIM:(h + 1) * HEAD_DIM] = acc_ref[h * t_new:(h + 1) * t_new, :]


def _attn_sample(q, q_blk, k, v, cache_k, cache_v, page_table, sb_bias, n_seq, t_new, n_heads,
                 pages=PAGES_PER_STEP):
    n_phys, page = cache_k.shape[0], cache_k.shape[1]
    n_pages = page_table.shape[1]
    assert n_pages % pages == 0
    att_w = n_heads * HEAD_DIM
    rows = page * n_heads
    new_rows = t_new * n_heads
    assert LANES % n_heads == 0 and rows % LANES == 0 and new_rows <= LANES and t_new % 8 == 0
    ck = cache_k.reshape(n_phys, rows, HEAD_DIM)
    cv = cache_v.reshape(n_phys, rows, HEAD_DIM)
    kn = k.reshape(n_seq, new_rows, HEAD_DIM)
    vn = v.reshape(n_seq, new_rows, HEAD_DIM)
    bias = jnp.broadcast_to(jnp.tile(sb_bias, rows // n_heads)[None, :], (t_new, rows))
    r = lax.broadcasted_iota(jnp.int32, (LANES, LANES), 0)
    c = lax.broadcasted_iota(jnp.int32, (LANES, LANES), 1)
    same_head = (r % n_heads) == (c % n_heads)
    tri2 = jnp.concatenate([same_head & (r // n_heads > c // n_heads), same_head], axis=1).astype(BF16)

    def page_spec(i):
        return pl.BlockSpec((None, rows, HEAD_DIM),
                            lambda b, s, pt: (pt[b, n_pages - 1 - (s * pages + i)], 0, 0))

    grid_spec = pltpu.PrefetchScalarGridSpec(
        num_scalar_prefetch=1,
        grid=(n_seq, n_pages // pages),
        in_specs=[pl.BlockSpec((LANES, 2 * LANES), lambda b, s, pt: (0, 0)),
                  pl.BlockSpec((t_new, rows), lambda b, s, pt: (0, 0)),
                  pl.BlockSpec((t_new, att_w), lambda b, s, pt: (b, q_blk)),
                  pl.BlockSpec((None, new_rows, HEAD_DIM), lambda b, s, pt: (b, 0, 0)),
                  pl.BlockSpec((None, new_rows, HEAD_DIM), lambda b, s, pt: (b, 0, 0))]
                 + [page_spec(i) for i in range(pages)] * 2,
        out_specs=pl.BlockSpec((t_new, att_w), lambda b, s, pt: (b, 0)),
        scratch_shapes=[pltpu.VMEM((n_heads * t_new, HEAD_DIM), F32),
                        pltpu.VMEM((t_new, LANES), F32),
                        pltpu.VMEM((n_heads * t_new, HEAD_DIM), F32)],
    )
    kern = functools.partial(_attn_sample_kernel, n_heads=n_heads, t_new=t_new, pages=pages)
    return pl.pallas_call(
        kern,
        grid_spec=grid_spec,
        out_shape=jax.ShapeDtypeStruct((n_seq * t_new, att_w), F32),
        compiler_params=_cparams(("parallel", "arbitrary")),
        name="attn_sample",
    )(page_table, tri2, bias, q, kn, vn, *([ck] * pages), *([cv] * pages))


def _pool_mix(u_ref, hist_ref, wg_ref, ps_ref, p_ref, ext_ref, pa_ref, pb_ref, *, G, R, t0, past_len, fresh):
    c_all = u_ref.shape[-1]
    cg = c_all // len(POOL_WINDOWS)
    n = POOL_HIST + R
    cur = u_ref[...].reshape(G, R, c_all)
    hist = hist_ref[...]
    if fresh:
        hist = jnp.where(t0 == 0, 0.0, hist)
    ext_ref[:, 0:POOL_HIST, :] = hist
    ext_ref[:, POOL_HIST:n, :] = cur

    src, dst = ext_ref, pa_ref
    for lvl, win in enumerate(POOL_WINDOWS):
        half = win // 2
        lo = lvl * cg
        rows_out = n - (win - 1)
        dst[:, 0:rows_out, lo:] = src[:, half:half + rows_out, lo:] + src[:, 0:rows_out, lo:]
        src, dst = dst, (pb_ref if dst is pa_ref else pa_ref)
        yield

    pos = past_len + t0 + lax.broadcasted_iota(jnp.int32, (1, R, cg), 1)
    for g, win in enumerate(POOL_WINDOWS):
        buf = pa_ref if g % 2 == 0 else pb_ref
        cols = slice(g * cg, (g + 1) * cg)
        first = POOL_HIST - (win - 1)
        wsum = buf[:, first:first + R, cols]
        cnt = jnp.minimum(win, pos + 1).astype(F32)
        pooled = wsum / cnt - cur[:, :, cols]
        mixed = _dot(pooled.reshape(G * R, cg), wg_ref[g])
        p_ref[:, cols] = (mixed * ps_ref[:, cols]).astype(p_ref.dtype)
        yield


def _mixer_kernel(a_ref, u_ref, hist_ref, gl_ref, x_ref, gt_ref, wg_ref, ps_ref, bg_ref, wa_ref, wp_ref, wo_ref,
                  gpost_ref, o_ref, p_ref, ext_ref, pa_ref, pb_ref, *, G, R, tiles_per_seq, past_len, fresh):
    d = x_ref.shape[-1]
    t0 = (pl.program_id(0) % tiles_per_seq) * R if fresh else 0
    stages = _pool_mix(u_ref, hist_ref, wg_ref, ps_ref, p_ref, ext_ref, pa_ref, pb_ref,
                       G=G, R=R, t0=t0, past_len=past_len, fresh=fresh)
    a = a_ref[...].astype(BF16)
    n_chunks = 2 * len(POOL_WINDOWS)
    cw = d // n_chunks
    parts = []
    for c in range(n_chunks):
        cols = slice(c * cw, (c + 1) * cw)
        parts.append(jax.nn.sigmoid(gl_ref[:, cols] + bg_ref[:, cols]) * _dot(a, wa_ref[:, cols]))
        next(stages, None)
    for _ in stages:
        pass
    merged_a = jnp.concatenate(parts, axis=1)
    merged = merged_a + jax.nn.sigmoid(gl_ref[:, d:] + bg_ref[:, d:]) * _dot(p_ref[...], wp_ref[...])
    y = _dot(merged.astype(BF16), wo_ref[...])
    o_ref[...] = x_ref[...] + gt_ref[...] * _rms(y, gpost_ref[...]).reshape(x_ref.shape)


def _mixer(rows, a, glu, u_blk, hist3, hist_map, x3, mod3, w_pool_group, pool_scale, b_gate, w_a, w_p, w_o, g_post,
           past_len, fresh):
    d = x3.shape[-1]
    n_groups, cg = w_pool_group.shape[0], w_pool_group.shape[1]
    pool_w, att_w = pool_scale.shape[0], a.shape[1]
    G, R, tm = rows.G, rows.R, rows.tm
    const = lambda shape: pl.BlockSpec(shape, lambda i: (0,) * len(shape), pipeline_mode=pl.Buffered(1))
    ext = pltpu.VMEM((G, POOL_HIST + R, pool_w), F32)
    kern = functools.partial(_mixer_kernel, G=G, R=R, tiles_per_seq=rows.tiles_per_seq,
                             past_len=past_len, fresh=fresh)
    return pl.pallas_call(
        kern,
        grid=(rows.n_tiles,),
        in_specs=[pl.BlockSpec((tm, att_w), lambda i: (i, 0)),
                  pl.BlockSpec((tm, pool_w), lambda i: (i, u_blk)),
                  pl.BlockSpec((G, POOL_HIST, pool_w), hist_map),
                  pl.BlockSpec((tm, 2 * d), lambda i: (i, 0)),
                  rows.x_spec(d, n_grid=1),
                  rows.mod_spec(d, 2, n_grid=1),
                  const((n_groups, cg, cg)), const((1, pool_w)), const((1, 2 * d)),
                  const((att_w, d)), const((pool_w, d)), const((d, d)), const((1, d))],
        out_specs=rows.x_spec(d, n_grid=1),
        out_shape=jax.ShapeDtypeStruct(x3.shape, F32),
        scratch_shapes=[pltpu.VMEM((tm, pool_w), BF16), ext, ext, ext],
        compiler_params=_cparams(("parallel",)),
        name="mixer",
    )(a, glu, hist3, glu, x3, mod3, w_pool_group, pool_scale.reshape(1, pool_w), b_gate.reshape(1, 2 * d),
      w_a, w_p, w_o, g_post.reshape(1, d))


def _ffn_kernel(x_ref, sh_ref, sc_ref, gt_ref, gpre_ref, gpost_ref, wg_ref, wu_ref, wo_ref, o_ref, *rest, emit):
    h_ref = rest[-1]
    j = pl.program_id(1)
    last = pl.num_programs(1) - 1
    wg, wu, wo = (r[...].astype(BF16) for r in (wg_ref, wu_ref, wo_ref))
    if emit:
        for ref, val in zip(rest[:3], (wg, wu, wo)):
            ref[...] = val

    def hidden(h):
        act = jax.nn.silu(_dot(h, wg)) * _dot(h, wu)
        return _dot(act.astype(BF16), wo)

    chunks = _row_chunks(x_ref.shape)
    flat = lambda rows: rows if x_ref.shape[0] == 1 else slice(None)

    @pl.when(j == 0)
    def _():
        for rows in chunks:
            h = _rms(x_ref[:, rows, :], gpre_ref[...]) * (1.0 + sc_ref[...]) + sh_ref[...]
            h = h.reshape(-1, h.shape[-1]).astype(h_ref.dtype)
            h_ref[flat(rows), :] = h
            o_ref[:, rows, :] = hidden(h).reshape(o_ref[:, rows, :].shape)

    @pl.when((j > 0) & (j < last))
    def _():
        o_ref[...] += hidden(h_ref[...]).reshape(o_ref.shape)

    @pl.when(j == last)
    def _():
        for rows in chunks:
            o = o_ref[:, rows, :]
            o = o + hidden(h_ref[flat(rows), :]).reshape(o.shape)
            o_ref[:, rows, :] = x_ref[:, rows, :] + gt_ref[...] * _rms(o, gpost_ref[...])


def _ffn(rows, x3, mod3, g_pre, g_post, wg, wu, wo, wu_col0, tf, emit):
    d = x3.shape[-1]
    d_ff = wo.shape[0]
    nf = d_ff // tf
    assert nf * tf == d_ff and nf >= 2 and wu_col0 % tf == 0
    u0 = wu_col0 // tf
    w_specs = [pl.BlockSpec((d, tf), lambda i, j: (0, j)),
               pl.BlockSpec((d, tf), lambda i, j: (0, u0 + j)),
               pl.BlockSpec((tf, d), lambda i, j: (j, 0))]
    emit_specs = [pl.BlockSpec((d, tf), lambda i, j: (0, j)),
                  pl.BlockSpec((d, tf), lambda i, j: (0, j)),
                  pl.BlockSpec((tf, d), lambda i, j: (j, 0))]
    emit_shapes = [jax.ShapeDtypeStruct((d, d_ff), BF16), jax.ShapeDtypeStruct((d, d_ff), BF16),
                   jax.ShapeDtypeStruct((d_ff, d), BF16)]
    return pl.pallas_call(
        functools.partial(_ffn_kernel, emit=emit),
        grid=(rows.n_tiles, nf),
        in_specs=[rows.x_spec(d, pipeline_mode=pl.Buffered(1)),
                  rows.mod_spec(d, 3), rows.mod_spec(d, 4), rows.mod_spec(d, 5),
                  pl.BlockSpec((1, d), lambda i, j: (0, 0)),
                  pl.BlockSpec((1, d), lambda i, j: (0, 0))] + w_specs,
        out_specs=[rows.x_spec(d)] + (emit_specs if emit else []),
        out_shape=[jax.ShapeDtypeStruct(x3.shape, F32)] + (emit_shapes if emit else []),
        scratch_shapes=[pltpu.VMEM((rows.tm, d), BF16)],
        compiler_params=_cparams(("parallel", "arbitrary")),
        name="ffn",
    )(x3, mod3, mod3, mod3, g_pre.reshape(1, d), g_post.reshape(1, d), wg, wu, wo)


def kernel(x_prompt, x_sample, c_prompt, c_sample, cache_k, cache_v, page_table, state_pool, w_cond, b_cond, g_mix_pre, g_mix_post, g_ffn_pre, g_ffn_post, w_in, b_gate, sb_bias, w_pool_group, pool_scale, w_branch_att, w_branch_pool, w_out, w_ffn_in, w_ffn_out):
    depth = w_cond.shape[0]
    b_p, t_p, d = x_prompt.shape
    b_s, t_s, _ = x_sample.shape
    n_heads = sb_bias.shape[1]
    att_w = n_heads * HEAD_DIM
    pool_w = pool_scale.shape[1]
    d_ff = w_ffn_out.shape[1]
    past_len = page_table.shape[1] * cache_k.shape[2]
    gl_w = w_in.shape[2] - 3 * att_w - pool_w
    assert gl_w % att_w == 0 and (gl_w + att_w) % pool_w == 0
    segments = ((att_w, 0, gl_w), (att_w, 1, 0), (att_w, 2, 0), (pool_w, 0, gl_w + att_w), (gl_w, 0, 0))
    out_widths = (gl_w + att_w + pool_w, att_w, att_w)
    q_blk, u_blk, u_col = gl_w // att_w, (gl_w + att_w) // pool_w, gl_w + att_w
    rows_p = lambda tile: _Rows(b_p, t_p, tile, mod_row0=b_s)
    rows_s = lambda tile: _Rows(b_s, t_s, tile, mod_row0=0)
    pad = (-(b_s + b_p)) % 8
    c_all = jnp.concatenate([c_sample, c_prompt, jnp.zeros((pad, d), F32)], axis=0)

    yp, ys = x_prompt, x_sample
    outs = [[] for _ in range(6)]
    for l in range(depth):
        mod = _modulation(c_all, w_cond[l], b_cond[l])
        mod3 = mod.reshape(mod.shape[0], 1, mod.shape[1])

        rs, rp = rows_s(TILE_INPROJ), rows_p(TILE_INPROJ)
        glu_s, ks, vs, w_in_b = _inproj(rs, rs.view(ys), mod3, g_mix_pre[l], w_in[l], segments, out_widths,
                                        tn=COLS_SHORT[0], emit=True)
        glu_p, kp, vp = _inproj(rp, rp.view(yp), mod3, g_mix_pre[l], w_in_b, segments, out_widths,
                                tn=COLS_LONG[0], emit=False)

        a_p = _attn_prompt(glu_p, q_blk, kp, vp, sb_bias[l], b_p, t_p, n_heads)
        a_s = _attn_sample(glu_s, q_blk, ks, vs, cache_k[l], cache_v[l], page_table, sb_bias[l], b_s, t_s, n_heads)

        w_a, w_p, w_o = (w.astype(BF16) for w in (w_branch_att[l], w_branch_pool[l], w_out[l]))
        rs, rp = rows_s(TILE_MIXER), rows_p(TILE_MIXER)
        hist_s = jnp.pad(state_pool[l], ((0, 0), (POOL_HIST - POOL_BUF, 0), (0, 0)))
        ys = _mixer(rs, a_s, glu_s, u_blk, hist_s, lambda i: (0, 0, 0), rs.view(ys), mod3, w_pool_group[l],
                    pool_scale[l], b_gate[l], w_a, w_p, w_o, g_mix_post[l], past_len=past_len, fresh=False)
        per_tile = rp.R // POOL_HIST
        yp = _mixer(rp, a_p, glu_p, u_blk, glu_p.reshape(glu_p.shape[0] // POOL_HIST, POOL_HIST, glu_p.shape[1]),
                    lambda i: (jnp.maximum(i * per_tile - 1, 0), 0, u_blk), rp.view(yp), mod3, w_pool_group[l],
                    pool_scale[l], b_gate[l], w_a, w_p, w_o, g_mix_post[l], past_len=0, fresh=True)

        rs, rp = rows_s(TILE_FFN), rows_p(TILE_FFN)
        ys, wg_b, wu_b, wo_b = _ffn(rs, rs.view(ys), mod3, g_ffn_pre[l], g_ffn_post[l], w_ffn_in[l], w_ffn_in[l],
                                    w_ffn_out[l], wu_col0=d_ff, tf=COLS_SHORT[1], emit=True)
        yp, = _ffn(rp, rp.view(yp), mod3, g_ffn_pre[l], g_ffn_post[l], wg_b, wu_b, wo_b,
                   wu_col0=0, tf=COLS_LONG[1], emit=False)
        yp, ys = yp.reshape(b_p, t_p, d), ys.reshape(b_s, t_s, d)

        up3 = glu_p.reshape(b_p, t_p, -1)[:, t_p - POOL_BUF:, u_col:u_col + pool_w]
        us3 = glu_s.reshape(b_s, t_s, -1)[:, :, u_col:u_col + pool_w]
        new = (kp.reshape(b_p, t_p, n_heads, HEAD_DIM), vp.reshape(b_p, t_p, n_heads, HEAD_DIM),
               up3,
               ks.reshape(b_s, t_s, n_heads, HEAD_DIM), vs.reshape(b_s, t_s, n_heads, HEAD_DIM),
               jnp.concatenate([state_pool[l], us3], axis=1)[:, -POOL_BUF:, :])
        for lst, val in zip(outs, new):
            lst.append(val)
    return (yp, ys) + tuple(jnp.stack(o) for o in outs)
```

```python
import functools

import jax
import jax.numpy as jnp
from jax import lax
from jax.experimental import pallas as pl
from jax.experimental.pallas import tpu as pltpu

F32 = jnp.float32
BF16 = jnp.bfloat16

HEAD_DIM = 128
LANES = 128
POOL_WINDOWS = (2, 4, 8, 16)
POOL_BUF = max(POOL_WINDOWS) - 1
POOL_HIST = POOL_BUF + 1
EPS = 1e-6
SB_SCALE = HEAD_DIM ** -0.5
LOG2E = 1.4426950408889634

VMEM_LIMIT_BYTES = 60 * 1024 * 1024
ATT_BLOCK = 256
PAGES_PER_STEP = 16
TILE_INPROJ = 1024
TILE_MIXER = 256
TILE_FFN = 1024
ROW_CHUNK = 256
COLS_LONG = (1024, 512)
COLS_SHORT = (1024, 512)


def _cparams(semantics):
    return pltpu.CompilerParams(dimension_semantics=semantics, vmem_limit_bytes=VMEM_LIMIT_BYTES)


def _dot(a, b):
    return jnp.dot(a, b, preferred_element_type=F32)


def _dot_nt(a, b):
    return lax.dot_general(a, b, (((1,), (1,)), ((), ())), preferred_element_type=F32)


def _rms(x, g):
    return x * lax.rsqrt(jnp.mean(x * x, axis=-1, keepdims=True) + EPS) * g


def _row_chunks(block_shape):
    g, r, _ = block_shape
    if g > 1 or r <= ROW_CHUNK:
        return [slice(None)]
    assert r % ROW_CHUNK == 0
    return [slice(c * ROW_CHUNK, (c + 1) * ROW_CHUNK) for c in range(r // ROW_CHUNK)]


def _mod_kernel(c_ref, w_ref, b_ref, o_ref):
    o_ref[...] = _dot(jax.nn.silu(c_ref[...]), w_ref[...]) + b_ref[...]


def _modulation(c_all, w_cond, b_cond, tn=1024):
    m, d = c_all.shape
    n = w_cond.shape[1]
    return pl.pallas_call(
        _mod_kernel,
        grid=(n // tn,),
        in_specs=[pl.BlockSpec((m, d), lambda j: (0, 0)),
                  pl.BlockSpec((d, tn), lambda j: (0, j)),
                  pl.BlockSpec((1, tn), lambda j: (0, j))],
        out_specs=pl.BlockSpec((m, tn), lambda j: (0, j)),
        out_shape=jax.ShapeDtypeStruct((m, n), F32),
        compiler_params=_cparams(("arbitrary",)),
        name="modulation",
    )(c_all, w_cond, b_cond.reshape(1, n))


class _Rows:
    def __init__(self, n_seq, seq_len, tile, mod_row0):
        if seq_len >= tile:
            assert seq_len % tile == 0
            self.G, self.R = 1, tile
            self.tiles_per_seq = seq_len // tile
            self.n_tiles = n_seq * self.tiles_per_seq
            self.mod_map = lambda i, k: (mod_row0 + i // self.tiles_per_seq, 0, k)
        else:
            assert seq_len % 8 == 0 and n_seq * seq_len <= tile
            self.G, self.R = n_seq, seq_len
            self.tiles_per_seq = 1
            self.n_tiles = 1
            mod_blk = mod_row0 // n_seq
            assert mod_blk * n_seq == mod_row0
            self.mod_map = lambda i, k: (mod_blk, 0, k)
        self.tm = self.G * self.R
        self.n_rows = n_seq * seq_len

    def x_spec(self, d, n_grid=2, **kw):
        if n_grid == 2:
            return pl.BlockSpec((self.G, self.R, d), lambda i, j: (i, 0, 0), **kw)
        return pl.BlockSpec((self.G, self.R, d), lambda i: (i, 0, 0), **kw)

    def mod_spec(self, d, k, n_grid=2):
        if n_grid == 2:
            return pl.BlockSpec((self.G, 1, d), lambda i, j: self.mod_map(i, k))
        return pl.BlockSpec((self.G, 1, d), lambda i: self.mod_map(i, k))

    def view(self, x):
        return x.reshape(self.n_tiles * self.G, self.R, x.shape[-1])


def _inproj_kernel(x_ref, sh_ref, sc_ref, g_ref, w_ref, *rest, starts, seg_out, n_out, emit):
    o_refs, h_ref = rest[:n_out], rest[-1]
    j = pl.program_id(1)
    w = w_ref[...].astype(BF16)
    if emit:
        rest[n_out][...] = w

    @pl.when(j == 0)
    def _():
        for rows in _row_chunks(x_ref.shape):
            h = _rms(x_ref[:, rows, :], g_ref[...]) * (1.0 + sc_ref[...]) + sh_ref[...]
            h = h.reshape(-1, h.shape[-1]).astype(h_ref.dtype)
            flat = rows if x_ref.shape[0] == 1 else slice(None)
            h_ref[flat, :] = h
            o_refs[seg_out[0]][flat, :] = _dot(h, w)

    for s, o in enumerate(seg_out):
        @pl.when((j >= max(starts[s], 1)) & (j < starts[s + 1]))
        def _(o_ref=o_refs[o]):
            o_ref[...] = _dot(h_ref[...], w)


def _inproj(rows, x3, mod3, g_pre, w_in, segments, out_widths, tn, emit):
    d = x3.shape[-1]
    assert sum(s[0] for s in segments) == w_in.shape[1]
    assert all(s[0] % tn == 0 and s[2] % tn == 0 for s in segments)
    starts = [0]
    for width, _, _ in segments:
        starts.append(starts[-1] + width // tn)
    n_steps = starts[-1]

    def out_spec(o):
        active = {}
        for s, (width, out, off) in enumerate(segments):
            if out == o:
                for t in range(width // tn):
                    active[starts[s] + t] = off // tn + t
        held, table = active[min(active)], []
        for step in range(n_steps):
            held = active.get(step, held)
            table.append(held)

        def col(j):
            c = jnp.int32(table[-1])
            for step in reversed(range(n_steps - 1)):
                c = jnp.where(j <= step, table[step], c)
            return c

        return pl.BlockSpec((rows.tm, tn), lambda i, j: (i, col(j)))

    w_spec = pl.BlockSpec((d, tn), lambda i, j: (0, j))
    kern = functools.partial(_inproj_kernel, starts=tuple(starts), seg_out=tuple(s[1] for s in segments),
                             n_out=len(out_widths), emit=emit)
    return pl.pallas_call(
        kern,
        grid=(rows.n_tiles, n_steps),
        in_specs=[rows.x_spec(d, pipeline_mode=pl.Buffered(1)),
                  rows.mod_spec(d, 0), rows.mod_spec(d, 1),
                  pl.BlockSpec((1, d), lambda i, j: (0, 0)),
                  w_spec],
        out_specs=[out_spec(o) for o in range(len(out_widths))] + ([w_spec] if emit else []),
        out_shape=[jax.ShapeDtypeStruct((rows.n_rows, w), F32) for w in out_widths]
                  + ([jax.ShapeDtypeStruct(w_in.shape, BF16)] if emit else []),
        scratch_shapes=[pltpu.VMEM((rows.tm, d), BF16)],
        compiler_params=_cparams(("parallel", "arbitrary")),
        name="inproj",
    )(x3, mod3, mod3, g_pre.reshape(1, d), w_in)


def _sb_logs(z, mask):
    t = jnp.log(1.0 + jnp.exp2(jnp.abs(z) * -LOG2E))
    log_beta = jnp.minimum(z, 0.0) - t
    neg_stay = z - log_beta
    if mask is not None:
        neg_stay = jnp.where(mask, neg_stay, 0.0)
    return log_beta, neg_stay


def _sb_weights(log_beta, neg_after, mask):
    w = jnp.exp2((log_beta - neg_after) * LOG2E)
    return w if mask is None else jnp.where(mask, w, 0.0)


def _tri(k):
    r = lax.broadcasted_iota(jnp.int32, (k, k), 0)
    c = lax.broadcasted_iota(jnp.int32, (k, k), 1)
    return (r > c).astype(BF16)


def _attn_prompt_kernel(bias_ref, tri_ref, q_ref, k_ref, v_ref, o_ref, qs_ref, c_ref, acc_ref, *, n_heads):
    blk = q_ref.shape[0]
    qi = pl.program_id(1)
    tri = tri_ref[...]
    qs_ref[...] = q_ref[...] * SB_SCALE

    def visit(start, mask, first):
        heads = range(n_heads)
        cols = [slice(h * HEAD_DIM, (h + 1) * HEAD_DIM) for h in heads]
        z = [_dot_nt(qs_ref[:, cols[h]], k_ref[pl.ds(start, blk), cols[h]]) + bias_ref[h] for h in heads]
        logs = [_sb_logs(z[h], mask) for h in heads]
        after = [_dot(logs[h][1].astype(BF16), tri) for h in heads]
        for h in heads:
            total = jnp.sum(logs[h][1], axis=-1, keepdims=True)
            if first:
                c_ref[h] = total
            else:
                after[h] = after[h] + c_ref[h]
                c_ref[h] += total
        w = [_sb_weights(logs[h][0], after[h], mask) for h in heads]
        for h in heads:
            pv = _dot(w[h], v_ref[pl.ds(start, blk), cols[h]])
            if first:
                acc_ref[h] = pv
            else:
                acc_ref[h] += pv

    r = lax.broadcasted_iota(jnp.int32, (blk, blk), 0)
    s = lax.broadcasted_iota(jnp.int32, (blk, blk), 1)
    visit(pl.multiple_of(qi * blk, blk), s < r, True)

    def body(it, _):
        visit(pl.multiple_of((qi - 1 - it) * blk, blk), None, False)
        return 0

    lax.fori_loop(0, qi, body, 0)
    for h in range(n_heads):
        o_ref[:, h * HEAD_DIM:(h + 1) * HEAD_DIM] = acc_ref[h]


def _attn_prompt(q, q_blk, k, v, sb_bias, n_seq, seq_len, n_heads, blk=ATT_BLOCK):
    nq = seq_len // blk
    att_w = n_heads * HEAD_DIM
    return pl.pallas_call(
        functools.partial(_attn_prompt_kernel, n_heads=n_heads),
        grid=(n_seq, nq),
        in_specs=[pl.BlockSpec(memory_space=pltpu.SMEM),
                  pl.BlockSpec((blk, blk), lambda b, i: (0, 0)),
                  pl.BlockSpec((blk, att_w), lambda b, i: (b * nq + i, q_blk)),
                  pl.BlockSpec((seq_len, att_w), lambda b, i: (b, 0)),
                  pl.BlockSpec((seq_len, att_w), lambda b, i: (b, 0))],
        out_specs=pl.BlockSpec((blk, att_w), lambda b, i: (b * nq + i, 0)),
        out_shape=jax.ShapeDtypeStruct((n_seq * seq_len, att_w), F32),
        scratch_shapes=[pltpu.VMEM((blk, att_w), F32),
                        pltpu.VMEM((n_heads, blk, 1), F32),
                        pltpu.VMEM((n_heads, blk, HEAD_DIM), F32)],
        compiler_params=_cparams(("parallel", "arbitrary")),
        name="attn_prompt",
    )(sb_bias, _tri(blk), q, k, v)


def _attn_sample_kernel(pt_ref, tri2_ref, bias_ref, q_ref, kn_ref, vn_ref, *rest, n_heads, t_new, pages):
    k_refs, v_refs = rest[:pages], rest[pages:2 * pages]
    o_ref, q2_ref, c_ref, acc_ref = rest[2 * pages:]
    s = pl.program_id(1)
    tri2 = tri2_ref[...]

    def head_masks(width):
        lane_head = lax.broadcasted_iota(jnp.int32, (t_new, width), 1) % n_heads
        return [lane_head == h for h in range(n_heads)]

    def lanes(x, j):
        return x[:, j * LANES:(j + 1) * LANES]

    def visit(kmats, vmats, bias, mask):
        hm = head_masks(kmats[0].shape[0])
        logs = []
        for km in kmats:
            zf = _dot_nt(q2_ref[...], km)
            z = zf[0:t_new]
            for h in range(1, n_heads):
                z = jnp.where(hm[h], zf[h * t_new:(h + 1) * t_new], z)
            logs.append(_sb_logs(z + bias, mask))
        n_ch = kmats[0].shape[0] // LANES
        order = [(i, j) for i in range(len(kmats)) for j in reversed(range(n_ch))]
        lhs = jnp.concatenate([lanes(logs[i][1], j) for i, j in order], axis=0)
        hi = lhs.astype(BF16)
        lo = (lhs - hi.astype(F32)).astype(BF16)
        st = _dot(hi, tri2) + _dot(lo, tri2)
        run = c_ref[...]
        after = {}
        for idx, ij in enumerate(order):
            blk = st[idx * t_new:(idx + 1) * t_new]
            after[ij] = blk[:, :LANES] + run
            run = run + blk[:, LANES:]
        c_ref[...] = run
        pv = None
        for i, vm in enumerate(vmats):
            aft = jnp.concatenate([after[(i, j)] for j in range(n_ch)], axis=1)
            w = _sb_weights(logs[i][0], aft, mask)
            we = jnp.concatenate([jnp.where(hm[h], w, 0.0) for h in range(n_heads)], axis=0)
            d = _dot(we, vm)
            pv = d if pv is None else pv + d
        acc_ref[...] += pv

    @pl.when(s == 0)
    def _():
        for h in range(n_heads):
            q2_ref[h * t_new:(h + 1) * t_new, :] = q_ref[:, h * HEAD_DIM:(h + 1) * HEAD_DIM] * SB_SCALE
        c_ref[...] = jnp.zeros_like(c_ref)
        acc_ref[...] = jnp.zeros_like(acc_ref)
        pad = jnp.zeros((LANES - kn_ref.shape[0], HEAD_DIM), F32)
        k_pos = lax.broadcasted_iota(jnp.int32, (t_new, LANES), 1) // n_heads
        q_pos = lax.broadcasted_iota(jnp.int32, (t_new, LANES), 0)
        visit([jnp.concatenate([kn_ref[...], pad], axis=0)], [jnp.concatenate([vn_ref[...], pad], axis=0)],
              bias_ref[:, :LANES], k_pos < q_pos)

    visit([r[...] for r in k_refs], [r[...] for r in v_refs], bias_ref[...], None)

    @pl.when(s == pl.num_programs(1) - 1)
    def _():
        for h in range(n_heads):
            o_ref[:, h * HEAD_DIM:(h + 1) * HEAD_DIM] = acc_ref[h * t_new:(h + 1) * t_new, :]


def _attn_sample(q, q_blk, k, v, cache_k, cache_v, page_table, sb_bias, n_seq, t_new, n_heads,
                 pages=PAGES_PER_STEP):
    n_phys, page = cache_k.shape[0], cache_k.shape[1]
    n_pages = page_table.shape[1]
    assert n_pages % pages == 0
    att_w = n_heads * HEAD_DIM
    rows = page * n_heads
    new_rows = t_new * n_heads
    assert LANES % n_heads == 0 and rows % LANES == 0 and new_rows <= LANES and t_new % 8 == 0
    ck = cache_k.reshape(n_phys, rows, HEAD_DIM)
    cv = cache_v.reshape(n_phys, rows, HEAD_DIM)
    kn = k.reshape(n_seq, new_rows, HEAD_DIM)
    vn = v.reshape(n_seq, new_rows, HEAD_DIM)
    bias = jnp.broadcast_to(jnp.tile(sb_bias, rows // n_heads)[None, :], (t_new, rows))
    r = lax.broadcasted_iota(jnp.int32, (LANES, LANES), 0)
    c = lax.broadcasted_iota(jnp.int32, (LANES, LANES), 1)
    same_head = (r % n_heads) == (c % n_heads)
    tri2 = jnp.concatenate([same_head & (r // n_heads > c // n_heads), same_head], axis=1).astype(BF16)

    def page_spec(i):
        return pl.BlockSpec((None, rows, HEAD_DIM),
                            lambda b, s, pt: (pt[b, n_pages - 1 - (s * pages + i)], 0, 0))

    grid_spec = pltpu.PrefetchScalarGridSpec(
        num_scalar_prefetch=1,
        grid=(n_seq, n_pages // pages),
        in_specs=[pl.BlockSpec((LANES, 2 * LANES), lambda b, s, pt: (0, 0)),
                  pl.BlockSpec((t_new, rows), lambda b, s, pt: (0, 0)),
                  pl.BlockSpec((t_new, att_w), lambda b, s, pt: (b, q_blk)),
                  pl.BlockSpec((None, new_rows, HEAD_DIM), lambda b, s, pt: (b, 0, 0)),
                  pl.BlockSpec((None, new_rows, HEAD_DIM), lambda b, s, pt: (b, 0, 0))]
                 + [page_spec(i) for i in range(pages)] * 2,
        out_specs=pl.BlockSpec((t_new, att_w), lambda b, s, pt: (b, 0)),
        scratch_shapes=[pltpu.VMEM((n_heads * t_new, HEAD_DIM), F32),
                        pltpu.VMEM((t_new, LANES), F32),
                        pltpu.VMEM((n_heads * t_new, HEAD_DIM), F32)],
    )
    kern = functools.partial(_attn_sample_kernel, n_heads=n_heads, t_new=t_new, pages=pages)
    return pl.pallas_call(
        kern,
        grid_spec=grid_spec,
        out_shape=jax.ShapeDtypeStruct((n_seq * t_new, att_w), F32),
        compiler_params=_cparams(("parallel", "arbitrary")),
        name="attn_sample",
    )(page_table, tri2, bias, q, kn, vn, *([ck] * pages), *([cv] * pages))


def _pool_mix(u_ref, hist_ref, wg_ref, ps_ref, p_ref, ext_ref, pa_ref, pb_ref, *, G, R, t0, past_len, fresh):
    c_all = u_ref.shape[-1]
    cg = c_all // len(POOL_WINDOWS)
    n = POOL_HIST + R
    cur = u_ref[...].reshape(G, R, c_all)
    hist = hist_ref[...]
    if fresh:
        hist = jnp.where(t0 == 0, 0.0, hist)
    ext_ref[:, 0:POOL_HIST, :] = hist
    ext_ref[:, POOL_HIST:n, :] = cur

    src, dst = ext_ref, pa_ref
    for lvl, win in enumerate(POOL_WINDOWS):
        half = win // 2
        lo = lvl * cg
        rows_out = n - (win - 1)
        dst[:, 0:rows_out, lo:] = src[:, half:half + rows_out, lo:] + src[:, 0:rows_out, lo:]
        src, dst = dst, (pb_ref if dst is pa_ref else pa_ref)
        yield

    pos = past_len + t0 + lax.broadcasted_iota(jnp.int32, (1, R, cg), 1)
    for g, win in enumerate(POOL_WINDOWS):
        buf = pa_ref if g % 2 == 0 else pb_ref
        cols = slice(g * cg, (g + 1) * cg)
        first = POOL_HIST - (win - 1)
        wsum = buf[:, first:first + R, cols]
        cnt = jnp.minimum(win, pos + 1).astype(F32)
        pooled = wsum / cnt - cur[:, :, cols]
        mixed = _dot(pooled.reshape(G * R, cg), wg_ref[g])
        p_ref[:, cols] = (mixed * ps_ref[:, cols]).astype(p_ref.dtype)
        yield


def _mixer_kernel(a_ref, u_ref, hist_ref, gl_ref, x_ref, gt_ref, wg_ref, ps_ref, bg_ref, wa_ref, wp_ref, wo_ref,
                  gpost_ref, o_ref, p_ref, ext_ref, pa_ref, pb_ref, *, G, R, tiles_per_seq, past_len, fresh):
    d = x_ref.shape[-1]
    t0 = (pl.program_id(0) % tiles_per_seq) * R if fresh else 0
    stages = _pool_mix(u_ref, hist_ref, wg_ref, ps_ref, p_ref, ext_ref, pa_ref, pb_ref,
                       G=G, R=R, t0=t0, past_len=past_len, fresh=fresh)
    a = a_ref[...].astype(BF16)
    n_chunks = 2 * len(POOL_WINDOWS)
    cw = d // n_chunks
    parts = []
    for c in range(n_chunks):
        cols = slice(c * cw, (c + 1) * cw)
        parts.append(jax.nn.sigmoid(gl_ref[:, cols] + bg_ref[:, cols]) * _dot(a, wa_ref[:, cols]))
        next(stages, None)
    for _ in stages:
        pass
    merged_a = jnp.concatenate(parts, axis=1)
    merged = merged_a + jax.nn.sigmoid(gl_ref[:, d:] + bg_ref[:, d:]) * _dot(p_ref[...], wp_ref[...])
    y = _dot(merged.astype(BF16), wo_ref[...])
    o_ref[...] = x_ref[...] + gt_ref[...] * _rms(y, gpost_ref[...]).reshape(x_ref.shape)


def _mixer(rows, a, glu, u_blk, hist3, hist_map, x3, mod3, w_pool_group, pool_scale, b_gate, w_a, w_p, w_o, g_post,
           past_len, fresh):
    d = x3.shape[-1]
    n_groups, cg = w_pool_group.shape[0], w_pool_group.shape[1]
    pool_w, att_w = pool_scale.shape[0], a.shape[1]
    G, R, tm = rows.G, rows.R, rows.tm
    const = lambda shape: pl.BlockSpec(shape, lambda i: (0,) * len(shape), pipeline_mode=pl.Buffered(1))
    ext = pltpu.VMEM((G, POOL_HIST + R, pool_w), F32)
    kern = functools.partial(_mixer_kernel, G=G, R=R, tiles_per_seq=rows.tiles_per_seq,
                             past_len=past_len, fresh=fresh)
    return pl.pallas_call(
        kern,
        grid=(rows.n_tiles,),
        in_specs=[pl.BlockSpec((tm, att_w), lambda i: (i, 0)),
                  pl.BlockSpec((tm, pool_w), lambda i: (i, u_blk)),
                  pl.BlockSpec((G, POOL_HIST, pool_w), hist_map),
                  pl.BlockSpec((tm, 2 * d), lambda i: (i, 0)),
                  rows.x_spec(d, n_grid=1),
                  rows.mod_spec(d, 2, n_grid=1),
                  const((n_groups, cg, cg)), const((1, pool_w)), const((1, 2 * d)),
                  const((att_w, d)), const((pool_w, d)), const((d, d)), const((1, d))],
        out_specs=rows.x_spec(d, n_grid=1),
        out_shape=jax.ShapeDtypeStruct(x3.shape, F32),
        scratch_shapes=[pltpu.VMEM((tm, pool_w), BF16), ext, ext, ext],
        compiler_params=_cparams(("parallel",)),
        name="mixer",
    )(a, glu, hist3, glu, x3, mod3, w_pool_group, pool_scale.reshape(1, pool_w), b_gate.reshape(1, 2 * d),
      w_a, w_p, w_o, g_post.reshape(1, d))


def _ffn_kernel(x_ref, sh_ref, sc_ref, gt_ref, gpre_ref, gpost_ref, wg_ref, wu_ref, wo_ref, o_ref, *rest, emit):
    h_ref = rest[-1]
    j = pl.program_id(1)
    last = pl.num_programs(1) - 1
    wg, wu, wo = (r[...].astype(BF16) for r in (wg_ref, wu_ref, wo_ref))
    if emit:
        for ref, val in zip(rest[:3], (wg, wu, wo)):
            ref[...] = val

    def hidden(h):
        act = jax.nn.silu(_dot(h, wg)) * _dot(h, wu)
        return _dot(act.astype(BF16), wo)

    chunks = _row_chunks(x_ref.shape)
    flat = lambda rows: rows if x_ref.shape[0] == 1 else slice(None)

    @pl.when(j == 0)
    def _():
        for rows in chunks:
            h = _rms(x_ref[:, rows, :], gpre_ref[...]) * (1.0 + sc_ref[...]) + sh_ref[...]
            h = h.reshape(-1, h.shape[-1]).astype(h_ref.dtype)
            h_ref[flat(rows), :] = h
            o_ref[:, rows, :] = hidden(h).reshape(o_ref[:, rows, :].shape)

    @pl.when((j > 0) & (j < last))
    def _():
        o_ref[...] += hidden(h_ref[...]).reshape(o_ref.shape)

    @pl.when(j == last)
    def _():
        for rows in chunks:
            o = o_ref[:, rows, :]
            o = o + hidden(h_ref[flat(rows), :]).reshape(o.shape)
            o_ref[:, rows, :] = x_ref[:, rows, :] + gt_ref[...] * _rms(o, gpost_ref[...])


def _ffn(rows, x3, mod3, g_pre, g_post, wg, wu, wo, wu_col0, tf, emit):
    d = x3.shape[-1]
    d_ff = wo.shape[0]
    nf = d_ff // tf
    assert nf * tf == d_ff and nf >= 2 and wu_col0 % tf == 0
    u0 = wu_col0 // tf
    w_specs = [pl.BlockSpec((d, tf), lambda i, j: (0, j)),
               pl.BlockSpec((d, tf), lambda i, j: (0, u0 + j)),
               pl.BlockSpec((tf, d), lambda i, j: (j, 0))]
    emit_specs = [pl.BlockSpec((d, tf), lambda i, j: (0, j)),
                  pl.BlockSpec((d, tf), lambda i, j: (0, j)),
                  pl.BlockSpec((tf, d), lambda i, j: (j, 0))]
    emit_shapes = [jax.ShapeDtypeStruct((d, d_ff), BF16), jax.ShapeDtypeStruct((d, d_ff), BF16),
                   jax.ShapeDtypeStruct((d_ff, d), BF16)]
    return pl.pallas_call(
        functools.partial(_ffn_kernel, emit=emit),
        grid=(rows.n_tiles, nf),
        in_specs=[rows.x_spec(d, pipeline_mode=pl.Buffered(1)),
                  rows.mod_spec(d, 3), rows.mod_spec(d, 4), rows.mod_spec(d, 5),
                  pl.BlockSpec((1, d), lambda i, j: (0, 0)),
                  pl.BlockSpec((1, d), lambda i, j: (0, 0))] + w_specs,
        out_specs=[rows.x_spec(d)] + (emit_specs if emit else []),
        out_shape=[jax.ShapeDtypeStruct(x3.shape, F32)] + (emit_shapes if emit else []),
        scratch_shapes=[pltpu.VMEM((rows.tm, d), BF16)],
        compiler_params=_cparams(("parallel", "arbitrary")),
        name="ffn",
    )(x3, mod3, mod3, mod3, g_pre.reshape(1, d), g_post.reshape(1, d), wg, wu, wo)


def kernel(x_prompt, x_sample, c_prompt, c_sample, cache_k, cache_v, page_table, state_pool, w_cond, b_cond, g_mix_pre, g_mix_post, g_ffn_pre, g_ffn_post, w_in, b_gate, sb_bias, w_pool_group, pool_scale, w_branch_att, w_branch_pool, w_out, w_ffn_in, w_ffn_out):
    depth = w_cond.shape[0]
    b_p, t_p, d = x_prompt.shape
    b_s, t_s, _ = x_sample.shape
    n_heads = sb_bias.shape[1]
    att_w = n_heads * HEAD_DIM
    pool_w = pool_scale.shape[1]
    d_ff = w_ffn_out.shape[1]
    past_len = page_table.shape[1] * cache_k.shape[2]
    gl_w = w_in.shape[2] - 3 * att_w - pool_w
    assert gl_w % att_w == 0 and (gl_w + att_w) % pool_w == 0
    segments = ((att_w, 0, gl_w), (att_w, 1, 0), (att_w, 2, 0), (pool_w, 0, gl_w + att_w), (gl_w, 0, 0))
    out_widths = (gl_w + att_w + pool_w, att_w, att_w)
    q_blk, u_blk, u_col = gl_w // att_w, (gl_w + att_w) // pool_w, gl_w + att_w
    rows_p = lambda tile: _Rows(b_p, t_p, tile, mod_row0=b_s)
    rows_s = lambda tile: _Rows(b_s, t_s, tile, mod_row0=0)
    pad = (-(b_s + b_p)) % 8
    c_all = jnp.concatenate([c_sample, c_prompt, jnp.zeros((pad, d), F32)], axis=0)

    yp, ys = x_prompt, x_sample
    outs = [[] for _ in range(6)]
    for l in range(depth):
        mod = _modulation(c_all, w_cond[l], b_cond[l])
        mod3 = mod.reshape(mod.shape[0], 1, mod.shape[1])

        rs, rp = rows_s(TILE_INPROJ), rows_p(TILE_INPROJ)
        glu_s, ks, vs, w_in_b = _inproj(rs, rs.view(ys), mod3, g_mix_pre[l], w_in[l], segments, out_widths,
                                        tn=COLS_SHORT[0], emit=True)
        glu_p, kp, vp = _inproj(rp, rp.view(yp), mod3, g_mix_pre[l], w_in_b, segments, out_widths,
                                tn=COLS_LONG[0], emit=False)

        a_p = _attn_prompt(glu_p, q_blk, kp, vp, sb_bias[l], b_p, t_p, n_heads)
        a_s = _attn_sample(glu_s, q_blk, ks, vs, cache_k[l], cache_v[l], page_table, sb_bias[l], b_s, t_s, n_heads)

        w_a, w_p, w_o = (w.astype(BF16) for w in (w_branch_att[l], w_branch_pool[l], w_out[l]))
        rs, rp = rows_s(TILE_MIXER), rows_p(TILE_MIXER)
        hist_s = jnp.pad(state_pool[l], ((0, 0), (POOL_HIST - POOL_BUF, 0), (0, 0)))
        ys = _mixer(rs, a_s, glu_s, u_blk, hist_s, lambda i: (0, 0, 0), rs.view(ys), mod3, w_pool_group[l],
                    pool_scale[l], b_gate[l], w_a, w_p, w_o, g_mix_post[l], past_len=past_len, fresh=False)
        per_tile = rp.R // POOL_HIST
        yp = _mixer(rp, a_p, glu_p, u_blk, glu_p.reshape(glu_p.shape[0] // POOL_HIST, POOL_HIST, glu_p.shape[1]),
                    lambda i: (jnp.maximum(i * per_tile - 1, 0), 0, u_blk), rp.view(yp), mod3, w_pool_group[l],
                    pool_scale[l], b_gate[l], w_a, w_p, w_o, g_mix_post[l], past_len=0, fresh=True)

        rs, rp = rows_s(TILE_FFN), rows_p(TILE_FFN)
        ys, wg_b, wu_b, wo_b = _ffn(rs, rs.view(ys), mod3, g_ffn_pre[l], g_ffn_post[l], w_ffn_in[l], w_ffn_in[l],
                                    w_ffn_out[l], wu_col0=d_ff, tf=COLS_SHORT[1], emit=True)
        yp, = _ffn(rp, rp.view(yp), mod3, g_ffn_pre[l], g_ffn_post[l], wg_b, wu_b, wo_b,
                   wu_col0=0, tf=COLS_LONG[1], emit=False)
        yp, ys = yp.reshape(b_p, t_p, d), ys.reshape(b_s, t_s, d)

        up3 = glu_p.reshape(b_p, t_p, -1)[:, t_p - POOL_BUF:, u_col:u_col + pool_w]
        us3 = glu_s.reshape(b_s, t_s, -1)[:, :, u_col:u_col + pool_w]
        new = (kp.reshape(b_p, t_p, n_heads, HEAD_DIM), vp.reshape(b_p, t_p, n_heads, HEAD_DIM),
               up3,
               ks.reshape(b_s, t_s, n_heads, HEAD_DIM), vs.reshape(b_s, t_s, n_heads, HEAD_DIM),
               jnp.concatenate([state_pool[l], us3], axis=1)[:, -POOL_BUF:, :])
        for lst, val in zip(outs, new):
            lst.append(val)
    return (yp, ys) + tuple(jnp.stack(o) for o in outs)
```

```python
import functools

import jax
import jax.numpy as jnp
from jax import lax
from jax.experimental import pallas as pl
from jax.experimental.pallas import tpu as pltpu

F32 = jnp.float32
BF16 = jnp.bfloat16

HEAD_DIM = 128
LANES = 128
POOL_WINDOWS = (2, 4, 8, 16)
POOL_BUF = max(POOL_WINDOWS) - 1
POOL_HIST = POOL_BUF + 1
EPS = 1e-6
SB_SCALE = HEAD_DIM ** -0.5
LOG2E = 1.4426950408889634

VMEM_LIMIT_BYTES = 60 * 1024 * 1024
ATT_BLOCK = 256
PAGES_PER_STEP = 16
TILE_INPROJ = 1024
TILE_MIXER = 256
TILE_FFN = 1024
ROW_CHUNK = 256
COLS_LONG = (1024, 512)
COLS_SHORT = (1024, 512)


def _cparams(semantics):
    return pltpu.CompilerParams(dimension_semantics=semantics, vmem_limit_bytes=VMEM_LIMIT_BYTES)


def _dot(a, b):
    return jnp.dot(a, b, preferred_element_type=F32)


def _dot_nt(a, b):
    return lax.dot_general(a, b, (((1,), (1,)), ((), ())), preferred_element_type=F32)


def _rms(x, g):
    return x * lax.rsqrt(jnp.mean(x * x, axis=-1, keepdims=True) + EPS) * g


def _row_chunks(block_shape):
    g, r, _ = block_shape
    if g > 1 or r <= ROW_CHUNK:
        return [slice(None)]
    assert r % ROW_CHUNK == 0
    return [slice(c * ROW_CHUNK, (c + 1) * ROW_CHUNK) for c in range(r // ROW_CHUNK)]


def _mod_kernel(c_ref, w_ref, b_ref, o_ref):
    o_ref[...] = _dot(jax.nn.silu(c_ref[...]), w_ref[...]) + b_ref[...]


def _modulation(c_all, w_cond, b_cond, tn=2048):
    m, d = c_all.shape
    n = w_cond.shape[1]
    return pl.pallas_call(
        _mod_kernel,
        grid=(n // tn,),
        in_specs=[pl.BlockSpec((m, d), lambda j: (0, 0)),
                  pl.BlockSpec((d, tn), lambda j: (0, j)),
                  pl.BlockSpec((1, tn), lambda j: (0, j))],
        out_specs=pl.BlockSpec((m, tn), lambda j: (0, j)),
        out_shape=jax.ShapeDtypeStruct((m, n), F32),
        compiler_params=_cparams(("arbitrary",)),
        name="modulation",
    )(c_all, w_cond, b_cond.reshape(1, n))


class _Rows:
    def __init__(self, n_seq, seq_len, tile, mod_row0):
        if seq_len >= tile:
            assert seq_len % tile == 0
            self.G, self.R = 1, tile
            self.tiles_per_seq = seq_len // tile
            self.n_tiles = n_seq * self.tiles_per_seq
            self.mod_map = lambda i, k: (mod_row0 + i // self.tiles_per_seq, 0, k)
        else:
            assert seq_len % 8 == 0 and n_seq * seq_len <= tile
            self.G, self.R = n_seq, seq_len
            self.tiles_per_seq = 1
            self.n_tiles = 1
            mod_blk = mod_row0 // n_seq
            assert mod_blk * n_seq == mod_row0
            self.mod_map = lambda i, k: (mod_blk, 0, k)
        self.tm = self.G * self.R
        self.n_rows = n_seq * seq_len

    def x_spec(self, d, n_grid=2, **kw):
        if n_grid == 2:
            return pl.BlockSpec((self.G, self.R, d), lambda i, j: (i, 0, 0), **kw)
        return pl.BlockSpec((self.G, self.R, d), lambda i: (i, 0, 0), **kw)

    def mod_spec(self, d, k, n_grid=2):
        if n_grid == 2:
            return pl.BlockSpec((self.G, 1, d), lambda i, j: self.mod_map(i, k))
        return pl.BlockSpec((self.G, 1, d), lambda i: self.mod_map(i, k))

    def view(self, x):
        return x.reshape(self.n_tiles * self.G, self.R, x.shape[-1])


def _inproj_kernel(x_ref, sh_ref, sc_ref, g_ref, w_ref, *rest, starts, seg_out, n_out, emit):
    o_refs, h_ref = rest[:n_out], rest[-1]
    j = pl.program_id(1)
    w = w_ref[...].astype(BF16)
    if emit:
        rest[n_out][...] = w

    @pl.when(j == 0)
    def _():
        for rows in _row_chunks(x_ref.shape):
            h = _rms(x_ref[:, rows, :], g_ref[...]) * (1.0 + sc_ref[...]) + sh_ref[...]
            h = h.reshape(-1, h.shape[-1]).astype(h_ref.dtype)
            flat = rows if x_ref.shape[0] == 1 else slice(None)
            h_ref[flat, :] = h
            o_refs[seg_out[0]][flat, :] = _dot(h, w)

    for s, o in enumerate(seg_out):
        @pl.when((j >= max(starts[s], 1)) & (j < starts[s + 1]))
        def _(o_ref=o_refs[o]):
            o_ref[...] = _dot(h_ref[...], w)


def _inproj(rows, x3, mod3, g_pre, w_in, segments, out_widths, tn, emit):
    d = x3.shape[-1]
    assert sum(s[0] for s in segments) == w_in.shape[1]
    assert all(s[0] % tn == 0 and s[2] % tn == 0 for s in segments)
    starts = [0]
    for width, _, _ in segments:
        starts.append(starts[-1] + width // tn)
    n_steps = starts[-1]

    def out_spec(o):
        active = {}
        for s, (width, out, off) in enumerate(segments):
            if out == o:
                for t in range(width // tn):
                    active[starts[s] + t] = off // tn + t
        held, table = active[min(active)], []
        for step in range(n_steps):
            held = active.get(step, held)
            table.append(held)

        def col(j):
            c = jnp.int32(table[-1])
            for step in reversed(range(n_steps - 1)):
                c = jnp.where(j <= step, table[step], c)
            return c

        return pl.BlockSpec((rows.tm, tn), lambda i, j: (i, col(j)))

    w_spec = pl.BlockSpec((d, tn), lambda i, j: (0, j))
    kern = functools.partial(_inproj_kernel, starts=tuple(starts), seg_out=tuple(s[1] for s in segments),
                             n_out=len(out_widths), emit=emit)
    return pl.pallas_call(
        kern,
        grid=(rows.n_tiles, n_steps),
        in_specs=[rows.x_spec(d, pipeline_mode=pl.Buffered(1)),
                  rows.mod_spec(d, 0), rows.mod_spec(d, 1),
                  pl.BlockSpec((1, d), lambda i, j: (0, 0)),
                  w_spec],
        out_specs=[out_spec(o) for o in range(len(out_widths))] + ([w_spec] if emit else []),
        out_shape=[jax.ShapeDtypeStruct((rows.n_rows, w), F32) for w in out_widths]
                  + ([jax.ShapeDtypeStruct(w_in.shape, BF16)] if emit else []),
        scratch_shapes=[pltpu.VMEM((rows.tm, d), BF16)],
        compiler_params=_cparams(("parallel", "arbitrary")),
        name="inproj",
    )(x3, mod3, mod3, g_pre.reshape(1, d), w_in)


def _sb_logs(z, mask):
    t = jnp.log(1.0 + jnp.exp2(jnp.abs(z) * -LOG2E))
    log_beta = jnp.minimum(z, 0.0) - t
    neg_stay = z - log_beta
    if mask is not None:
        neg_stay = jnp.where(mask, neg_stay, 0.0)
    return log_beta, neg_stay


def _sb_weights(log_beta, neg_after, mask):
    w = jnp.exp2((log_beta - neg_after) * LOG2E)
    return w if mask is None else jnp.where(mask, w, 0.0)


def _tri(k):
    r = lax.broadcasted_iota(jnp.int32, (k, k), 0)
    c = lax.broadcasted_iota(jnp.int32, (k, k), 1)
    return (r > c).astype(BF16)


def _attn_prompt_kernel(bias_ref, tri_ref, q_ref, k_ref, v_ref, o_ref, qs_ref, c_ref, acc_ref, *, n_heads):
    blk = q_ref.shape[0]
    qi = pl.program_id(1)
    tri = tri_ref[...]
    qs_ref[...] = q_ref[...] * SB_SCALE

    def visit(start, mask, first):
        heads = range(n_heads)
        cols = [slice(h * HEAD_DIM, (h + 1) * HEAD_DIM) for h in heads]
        z = [_dot_nt(qs_ref[:, cols[h]], k_ref[pl.ds(start, blk), cols[h]]) + bias_ref[h] for h in heads]
        logs = [_sb_logs(z[h], mask) for h in heads]
        after = [_dot(logs[h][1].astype(BF16), tri) for h in heads]
        for h in heads:
            total = jnp.sum(logs[h][1], axis=-1, keepdims=True)
            if first:
                c_ref[h] = total
            else:
                after[h] = after[h] + c_ref[h]
                c_ref[h] += total
        w = [_sb_weights(logs[h][0], after[h], mask) for h in heads]
        for h in heads:
            pv = _dot(w[h], v_ref[pl.ds(start, blk), cols[h]])
            if first:
                acc_ref[h] = pv
            else:
                acc_ref[h] += pv

    r = lax.broadcasted_iota(jnp.int32, (blk, blk), 0)
    s = lax.broadcasted_iota(jnp.int32, (blk, blk), 1)
    visit(pl.multiple_of(qi * blk, blk), s < r, True)

    def body(it, _):
        visit(pl.multiple_of((qi - 1 - it) * blk, blk), None, False)
        return 0

    lax.fori_loop(0, qi, body, 0)
    for h in range(n_heads):
        o_ref[:, h * HEAD_DIM:(h + 1) * HEAD_DIM] = acc_ref[h]


def _attn_prompt(q, q_blk, k, v, sb_bias, n_seq, seq_len, n_heads, blk=ATT_BLOCK):
    nq = seq_len // blk
    att_w = n_heads * HEAD_DIM
    return pl.pallas_call(
        functools.partial(_attn_prompt_kernel, n_heads=n_heads),
        grid=(n_seq, nq),
        in_specs=[pl.BlockSpec(memory_space=pltpu.SMEM),
                  pl.BlockSpec((blk, blk), lambda b, i: (0, 0)),
                  pl.BlockSpec((blk, att_w), lambda b, i: (b * nq + i, q_blk)),
                  pl.BlockSpec((seq_len, att_w), lambda b, i: (b, 0)),
                  pl.BlockSpec((seq_len, att_w), lambda b, i: (b, 0))],
        out_specs=pl.BlockSpec((blk, att_w), lambda b, i: (b * nq + i, 0)),
        out_shape=jax.ShapeDtypeStruct((n_seq * seq_len, att_w), F32),
        scratch_shapes=[pltpu.VMEM((blk, att_w), F32),
                        pltpu.VMEM((n_heads, blk, 1), F32),
                        pltpu.VMEM((n_heads, blk, HEAD_DIM), F32)],
        compiler_params=_cparams(("parallel", "arbitrary")),
        name="attn_prompt",
    )(sb_bias, _tri(blk), q, k, v)


def _attn_sample_kernel(pt_ref, tri2_ref, bias_ref, q_ref, kn_ref, vn_ref, *rest, n_heads, t_new, pages):
    k_refs, v_refs = rest[:pages], rest[pages:2 * pages]
    o_ref, q2_ref, c_ref, acc_ref = rest[2 * pages:]
    s = pl.program_id(1)
    tri2 = tri2_ref[...]

    def head_masks(width):
        lane_head = lax.broadcasted_iota(jnp.int32, (t_new, width), 1) % n_heads
        return [lane_head == h for h in range(n_heads)]

    def lanes(x, j):
        return x[:, j * LANES:(j + 1) * LANES]

    def visit(kmats, vmats, bias, mask):
        hm = head_masks(kmats[0].shape[0])
        logs = []
        for km in kmats:
            zf = _dot_nt(q2_ref[...], km)
            z = zf[0:t_new]
            for h in range(1, n_heads):
                z = jnp.where(hm[h], zf[h * t_new:(h + 1) * t_new], z)
            logs.append(_sb_logs(z + bias, mask))
        n_ch = kmats[0].shape[0] // LANES
        order = [(i, j) for i in range(len(kmats)) for j in reversed(range(n_ch))]
        lhs = jnp.concatenate([lanes(logs[i][1], j) for i, j in order], axis=0)
        hi = lhs.astype(BF16)
        lo = (lhs - hi.astype(F32)).astype(BF16)
        st = _dot(hi, tri2) + _dot(lo, tri2)
        run = c_ref[...]
        after = {}
        for idx, ij in enumerate(order):
            blk = st[idx * t_new:(idx + 1) * t_new]
            after[ij] = blk[:, :LANES] + run
            run = run + blk[:, LANES:]
        c_ref[...] = run
        pv = None
        for i, vm in enumerate(vmats):
            aft = jnp.concatenate([after[(i, j)] for j in range(n_ch)], axis=1)
            w = _sb_weights(logs[i][0], aft, mask)
            we = jnp.concatenate([jnp.where(hm[h], w, 0.0) for h in range(n_heads)], axis=0)
            d = _dot(we, vm)
            pv = d if pv is None else pv + d
        acc_ref[...] += pv

    @pl.when(s == 0)
    def _():
        for h in range(n_heads):
            q2_ref[h * t_new:(h + 1) * t_new, :] = q_ref[:, h * HEAD_DIM:(h + 1) * HEAD_DIM] * SB_SCALE
        c_ref[...] = jnp.zeros_like(c_ref)
        acc_ref[...] = jnp.zeros_like(acc_ref)
        pad = jnp.zeros((LANES - kn_ref.shape[0], HEAD_DIM), F32)
        k_pos = lax.broadcasted_iota(jnp.int32, (t_new, LANES), 1) // n_heads
        q_pos = lax.broadcasted_iota(jnp.int32, (t_new, LANES), 0)
        visit([jnp.concatenate([kn_ref[...], pad], axis=0)], [jnp.concatenate([vn_ref[...], pad], axis=0)],
              bias_ref[:, :LANES], k_pos < q_pos)

    visit([r[...] for r in k_refs], [r[...] for r in v_refs], bias_ref[...], None)

    @pl.when(s == pl.num_programs(1) - 1)
    def _():
        for h in range(n_heads):
            o_ref[:, h * HEAD_DIM:(h + 1) * HEAD_DIM] = acc_ref[h * t_new:(h + 1) * t_new, :]


def _attn_sample(q, q_blk, k, v, cache_k, cache_v, page_table, sb_bias, n_seq, t_new, n_heads,
                 pages=PAGES_PER_STEP):
    n_phys, page = cache_k.shape[0], cache_k.shape[1]
    n_pages = page_table.shape[1]
    assert n_pages % pages == 0
    att_w = n_heads * HEAD_DIM
    rows = page * n_heads
    new_rows = t_new * n_heads
    assert LANES % n_heads == 0 and rows % LANES == 0 and new_rows <= LANES and t_new % 8 == 0
    ck = cache_k.reshape(n_phys, rows, HEAD_DIM)
    cv = cache_v.reshape(n_phys, rows, HEAD_DIM)
    kn = k.reshape(n_seq, new_rows, HEAD_DIM)
    vn = v.reshape(n_seq, new_rows, HEAD_DIM)
    bias = jnp.broadcast_to(jnp.tile(sb_bias, rows // n_heads)[None, :], (t_new, rows))
    r = lax.broadcasted_iota(jnp.int32, (LANES, LANES), 0)
    c = lax.broadcasted_iota(jnp.int32, (LANES, LANES), 1)
    same_head = (r % n_heads) == (c % n_heads)
    tri2 = jnp.concatenate([same_head & (r // n_heads > c // n_heads), same_head], axis=1).astype(BF16)

    def page_spec(i):
        return pl.BlockSpec((None, rows, HEAD_DIM),
                            lambda b, s, pt: (pt[b, n_pages - 1 - (s * pages + i)], 0, 0))

    grid_spec = pltpu.PrefetchScalarGridSpec(
        num_scalar_prefetch=1,
        grid=(n_seq, n_pages // pages),
        in_specs=[pl.BlockSpec((LANES, 2 * LANES), lambda b, s, pt: (0, 0)),
                  pl.BlockSpec((t_new, rows), lambda b, s, pt: (0, 0)),
                  pl.BlockSpec((t_new, att_w), lambda b, s, pt: (b, q_blk)),
                  pl.BlockSpec((None, new_rows, HEAD_DIM), lambda b, s, pt: (b, 0, 0)),
                  pl.BlockSpec((None, new_rows, HEAD_DIM), lambda b, s, pt: (b, 0, 0))]
                 + [page_spec(i) for i in range(pages)] * 2,
        out_specs=pl.BlockSpec((t_new, att_w), lambda b, s, pt: (b, 0)),
        scratch_shapes=[pltpu.VMEM((n_heads * t_new, HEAD_DIM), F32),
                        pltpu.VMEM((t_new, LANES), F32),
                        pltpu.VMEM((n_heads * t_new, HEAD_DIM), F32)],
    )
    kern = functools.partial(_attn_sample_kernel, n_heads=n_heads, t_new=t_new, pages=pages)
    return pl.pallas_call(
        kern,
        grid_spec=grid_spec,
        out_shape=jax.ShapeDtypeStruct((n_seq * t_new, att_w), F32),
        compiler_params=_cparams(("parallel", "arbitrary")),
        name="attn_sample",
    )(page_table, tri2, bias, q, kn, vn, *([ck] * pages), *([cv] * pages))


def _pool_mix(u_ref, hist_ref, wg_ref, ps_ref, p_ref, ext_ref, pa_ref, pb_ref, *, G, R, t0, past_len, fresh):
    c_all = u_ref.shape[-1]
    cg = c_all // len(POOL_WINDOWS)
    n = POOL_HIST + R
    cur = u_ref[...].reshape(G, R, c_all)
    hist = hist_ref[...]
    if fresh:
        hist = jnp.where(t0 == 0, 0.0, hist)
    ext_ref[:, 0:POOL_HIST, :] = hist
    ext_ref[:, POOL_HIST:n, :] = cur

    src, dst = ext_ref, pa_ref
    for lvl, win in enumerate(POOL_WINDOWS):
        half = win // 2
        lo = lvl * cg
        rows_out = n - (win - 1)
        dst[:, 0:rows_out, lo:] = src[:, half:half + rows_out, lo:] + src[:, 0:rows_out, lo:]
        src, dst = dst, (pb_ref if dst is pa_ref else pa_ref)
        yield

    pos = past_len + t0 + lax.broadcasted_iota(jnp.int32, (1, R, cg), 1)
    for g, win in enumerate(POOL_WINDOWS):
        buf = pa_ref if g % 2 == 0 else pb_ref
        cols = slice(g * cg, (g + 1) * cg)
        first = POOL_HIST - (win - 1)
        wsum = buf[:, first:first + R, cols]
        cnt = jnp.minimum(win, pos + 1).astype(F32)
        pooled = wsum / cnt - cur[:, :, cols]
        mixed = _dot(pooled.reshape(G * R, cg), wg_ref[g])
        p_ref[:, cols] = (mixed * ps_ref[:, cols]).astype(p_ref.dtype)
        yield


def _mixer_kernel(a_ref, u_ref, hist_ref, gl_ref, x_ref, gt_ref, wg_ref, ps_ref, bg_ref, wa_ref, wp_ref, wo_ref,
                  gpost_ref, o_ref, p_ref, ext_ref, pa_ref, pb_ref, *, G, R, tiles_per_seq, past_len, fresh):
    d = x_ref.shape[-1]
    t0 = (pl.program_id(0) % tiles_per_seq) * R if fresh else 0
    stages = _pool_mix(u_ref, hist_ref, wg_ref, ps_ref, p_ref, ext_ref, pa_ref, pb_ref,
                       G=G, R=R, t0=t0, past_len=past_len, fresh=fresh)
    a = a_ref[...].astype(BF16)
    n_chunks = 2 * len(POOL_WINDOWS)
    cw = d // n_chunks
    parts = []
    for c in range(n_chunks):
        cols = slice(c * cw, (c + 1) * cw)
        parts.append(jax.nn.sigmoid(gl_ref[:, cols] + bg_ref[:, cols]) * _dot(a, wa_ref[:, cols]))
        next(stages, None)
    for _ in stages:
        pass
    merged_a = jnp.concatenate(parts, axis=1)
    merged = merged_a + jax.nn.sigmoid(gl_ref[:, d:] + bg_ref[:, d:]) * _dot(p_ref[...], wp_ref[...])
    y = _dot(merged.astype(BF16), wo_ref[...])
    o_ref[...] = x_ref[...] + gt_ref[...] * _rms(y, gpost_ref[...]).reshape(x_ref.shape)


def _mixer(rows, a, glu, u_blk, hist3, hist_map, x3, mod3, w_pool_group, pool_scale, b_gate, w_a, w_p, w_o, g_post,
           past_len, fresh):
    d = x3.shape[-1]
    n_groups, cg = w_pool_group.shape[0], w_pool_group.shape[1]
    pool_w, att_w = pool_scale.shape[0], a.shape[1]
    G, R, tm = rows.G, rows.R, rows.tm
    const = lambda shape: pl.BlockSpec(shape, lambda i: (0,) * len(shape), pipeline_mode=pl.Buffered(1))
    ext = pltpu.VMEM((G, POOL_HIST + R, pool_w), F32)
    kern = functools.partial(_mixer_kernel, G=G, R=R, tiles_per_seq=rows.tiles_per_seq,
                             past_len=past_len, fresh=fresh)
    return pl.pallas_call(
        kern,
        grid=(rows.n_tiles,),
        in_specs=[pl.BlockSpec((tm, att_w), lambda i: (i, 0)),
                  pl.BlockSpec((tm, pool_w), lambda i: (i, u_blk)),
                  pl.BlockSpec((G, POOL_HIST, pool_w), hist_map),
                  pl.BlockSpec((tm, 2 * d), lambda i: (i, 0)),
                  rows.x_spec(d, n_grid=1),
                  rows.mod_spec(d, 2, n_grid=1),
                  const((n_groups, cg, cg)), const((1, pool_w)), const((1, 2 * d)),
                  const((att_w, d)), const((pool_w, d)), const((d, d)), const((1, d))],
        out_specs=rows.x_spec(d, n_grid=1),
        out_shape=jax.ShapeDtypeStruct(x3.shape, F32),
        scratch_shapes=[pltpu.VMEM((tm, pool_w), BF16), ext, ext, ext],
        compiler_params=_cparams(("parallel",)),
        name="mixer",
    )(a, glu, hist3, glu, x3, mod3, w_pool_group, pool_scale.reshape(1, pool_w), b_gate.reshape(1, 2 * d),
      w_a, w_p, w_o, g_post.reshape(1, d))


def _ffn_kernel(x_ref, sh_ref, sc_ref, gt_ref, gpre_ref, gpost_ref, wg_ref, wu_ref, wo_ref, o_ref, *rest, emit):
    h_ref = rest[-1]
    j = pl.program_id(1)
    last = pl.num_programs(1) - 1
    wg, wu, wo = (r[...].astype(BF16) for r in (wg_ref, wu_ref, wo_ref))
    if emit:
        for ref, val in zip(rest[:3], (wg, wu, wo)):
            ref[...] = val

    def hidden(h):
        act = jax.nn.silu(_dot(h, wg)) * _dot(h, wu)
        return _dot(act.astype(BF16), wo)

    chunks = _row_chunks(x_ref.shape)
    flat = lambda rows: rows if x_ref.shape[0] == 1 else slice(None)

    @pl.when(j == 0)
    def _():
        for rows in chunks:
            h = _rms(x_ref[:, rows, :], gpre_ref[...]) * (1.0 + sc_ref[...]) + sh_ref[...]
            h = h.reshape(-1, h.shape[-1]).astype(h_ref.dtype)
            h_ref[flat(rows), :] = h
            o_ref[:, rows, :] = hidden(h).reshape(o_ref[:, rows, :].shape)

    @pl.when((j > 0) & (j < last))
    def _():
        o_ref[...] += hidden(h_ref[...]).reshape(o_ref.shape)

    @pl.when(j == last)
    def _():
        for rows in chunks:
            o = o_ref[:, rows, :]
            o = o + hidden(h_ref[flat(rows), :]).reshape(o.shape)
            o_ref[:, rows, :] = x_ref[:, rows, :] + gt_ref[...] * _rms(o, gpost_ref[...])


def _ffn(rows, x3, mod3, g_pre, g_post, wg, wu, wo, wu_col0, tf, emit):
    d = x3.shape[-1]
    d_ff = wo.shape[0]
    nf = d_ff // tf
    assert nf * tf == d_ff and nf >= 2 and wu_col0 % tf == 0
    u0 = wu_col0 // tf
    w_specs = [pl.BlockSpec((d, tf), lambda i, j: (0, j)),
               pl.BlockSpec((d, tf), lambda i, j: (0, u0 + j)),
               pl.BlockSpec((tf, d), lambda i, j: (j, 0))]
    emit_specs = [pl.BlockSpec((d, tf), lambda i, j: (0, j)),
                  pl.BlockSpec((d, tf), lambda i, j: (0, j)),
                  pl.BlockSpec((tf, d), lambda i, j: (j, 0))]
    emit_shapes = [jax.ShapeDtypeStruct((d, d_ff), BF16), jax.ShapeDtypeStruct((d, d_ff), BF16),
                   jax.ShapeDtypeStruct((d_ff, d), BF16)]
    return pl.pallas_call(
        functools.partial(_ffn_kernel, emit=emit),
        grid=(rows.n_tiles, nf),
        in_specs=[rows.x_spec(d, pipeline_mode=pl.Buffered(1)),
                  rows.mod_spec(d, 3), rows.mod_spec(d, 4), rows.mod_spec(d, 5),
                  pl.BlockSpec((1, d), lambda i, j: (0, 0)),
                  pl.BlockSpec((1, d), lambda i, j: (0, 0))] + w_specs,
        out_specs=[rows.x_spec(d)] + (emit_specs if emit else []),
        out_shape=[jax.ShapeDtypeStruct(x3.shape, F32)] + (emit_shapes if emit else []),
        scratch_shapes=[pltpu.VMEM((rows.tm, d), BF16)],
        compiler_params=_cparams(("parallel", "arbitrary")),
        name="ffn",
    )(x3, mod3, mod3, mod3, g_pre.reshape(1, d), g_post.reshape(1, d), wg, wu, wo)


def kernel(x_prompt, x_sample, c_prompt, c_sample, cache_k, cache_v, page_table, state_pool, w_cond, b_cond, g_mix_pre, g_mix_post, g_ffn_pre, g_ffn_post, w_in, b_gate, sb_bias, w_pool_group, pool_scale, w_branch_att, w_branch_pool, w_out, w_ffn_in, w_ffn_out):
    depth = w_cond.shape[0]
    b_p, t_p, d = x_prompt.shape
    b_s, t_s, _ = x_sample.shape
    n_heads = sb_bias.shape[1]
    att_w = n_heads * HEAD_DIM
    pool_w = pool_scale.shape[1]
    d_ff = w_ffn_out.shape[1]
    past_len = page_table.shape[1] * cache_k.shape[2]
    gl_w = w_in.shape[2] - 3 * att_w - pool_w
    assert gl_w % att_w == 0 and (gl_w + att_w) % pool_w == 0
    segments = ((att_w, 0, gl_w), (att_w, 1, 0), (att_w, 2, 0), (pool_w, 0, gl_w + att_w), (gl_w, 0, 0))
    out_widths = (gl_w + att_w + pool_w, att_w, att_w)
    q_blk, u_blk, u_col = gl_w // att_w, (gl_w + att_w) // pool_w, gl_w + att_w
    rows_p = lambda tile: _Rows(b_p, t_p, tile, mod_row0=b_s)
    rows_s = lambda tile: _Rows(b_s, t_s, tile, mod_row0=0)
    pad = (-(b_s + b_p)) % 8
    c_all = jnp.concatenate([c_sample, c_prompt, jnp.zeros((pad, d), F32)], axis=0)

    yp, ys = x_prompt, x_sample
    outs = [[] for _ in range(6)]
    for l in range(depth):
        mod = _modulation(c_all, w_cond[l], b_cond[l])
        mod3 = mod.reshape(mod.shape[0], 1, mod.shape[1])

        rs, rp = rows_s(TILE_INPROJ), rows_p(TILE_INPROJ)
        glu_s, ks, vs, w_in_b = _inproj(rs, rs.view(ys), mod3, g_mix_pre[l], w_in[l], segments, out_widths,
                                        tn=COLS_SHORT[0], emit=True)
        glu_p, kp, vp = _inproj(rp, rp.view(yp), mod3, g_mix_pre[l], w_in_b, segments, out_widths,
                                tn=COLS_LONG[0], emit=False)

        a_p = _attn_prompt(glu_p, q_blk, kp, vp, sb_bias[l], b_p, t_p, n_heads)
        a_s = _attn_sample(glu_s, q_blk, ks, vs, cache_k[l], cache_v[l], page_table, sb_bias[l], b_s, t_s, n_heads)

        w_a, w_p, w_o = (w.astype(BF16) for w in (w_branch_att[l], w_branch_pool[l], w_out[l]))
        rs, rp = rows_s(TILE_MIXER), rows_p(TILE_MIXER)
        hist_s = jnp.pad(state_pool[l], ((0, 0), (POOL_HIST - POOL_BUF, 0), (0, 0)))
        ys = _mixer(rs, a_s, glu_s, u_blk, hist_s, lambda i: (0, 0, 0), rs.view(ys), mod3, w_pool_group[l],
                    pool_scale[l], b_gate[l], w_a, w_p, w_o, g_mix_post[l], past_len=past_len, fresh=False)
        per_tile = rp.R // POOL_HIST
        yp = _mixer(rp, a_p, glu_p, u_blk, glu_p.reshape(glu_p.shape[0] // POOL_HIST, POOL_HIST, glu_p.shape[1]),
                    lambda i: (jnp.maximum(i * per_tile - 1, 0), 0, u_blk), rp.view(yp), mod3, w_pool_group[l],
                    pool_scale[l], b_gate[l], w_a, w_p, w_o, g_mix_post[l], past_len=0, fresh=True)

        rs, rp = rows_s(TILE_FFN), rows_p(TILE_FFN)
        ys, wg_b, wu_b, wo_b = _ffn(rs, rs.view(ys), mod3, g_ffn_pre[l], g_ffn_post[l], w_ffn_in[l], w_ffn_in[l],
                                    w_ffn_out[l], wu_col0=d_ff, tf=COLS_SHORT[1], emit=True)
        yp, = _ffn(rp, rp.view(yp), mod3, g_ffn_pre[l], g_ffn_post[l], wg_b, wu_b, wo_b,
                   wu_col0=0, tf=COLS_LONG[1], emit=False)
        yp, ys = yp.reshape(b_p, t_p, d), ys.reshape(b_s, t_s, d)

        up3 = glu_p.reshape(b_p, t_p, -1)[:, t_p - POOL_BUF:, u_col:u_col + pool_w]
        us3 = glu_s.reshape(b_s, t_s, -1)[:, :, u_col:u_col + pool_w]
        new = (kp.reshape(b_p, t_p, n_heads, HEAD_DIM), vp.reshape(b_p, t_p, n_heads, HEAD_DIM),
               up3,
               ks.reshape(b_s, t_s, n_heads, HEAD_DIM), vs.reshape(b_s, t_s, n_heads, HEAD_DIM),
               jnp.concatenate([state_pool[l], us3], axis=1)[:, -POOL_BUF:, :])
        for lst, val in zip(outs, new):
            lst.append(val)
    return (yp, ys) + tuple(jnp.stack(o) for o in outs)
```

```python
import functools

import jax
import jax.numpy as jnp
from jax import lax
from jax.experimental import pallas as pl
from jax.experimental.pallas import tpu as pltpu

F32 = jnp.float32
BF16 = jnp.bfloat16

HEAD_DIM = 128
LANES = 128
POOL_WINDOWS = (2, 4, 8, 16)
POOL_BUF = max(POOL_WINDOWS) - 1
POOL_HIST = POOL_BUF + 1
EPS = 1e-6
SB_SCALE = HEAD_DIM ** -0.5
LOG2E = 1.4426950408889634

VMEM_LIMIT_BYTES = 60 * 1024 * 1024
ATT_BLOCK = 256
PAGES_PER_STEP = 16
TILE_INPROJ = 1024
TILE_MIXER = 256
TILE_FFN = 1024
ROW_CHUNK = 256
COLS_LONG = (1024, 512)
COLS_SHORT = (1024, 512)


def _cparams(semantics):
    return pltpu.CompilerParams(dimension_semantics=semantics, vmem_limit_bytes=VMEM_LIMIT_BYTES)


def _dot(a, b):
    return jnp.dot(a, b, preferred_element_type=F32)


def _dot_nt(a, b):
    return lax.dot_general(a, b, (((1,), (1,)), ((), ())), preferred_element_type=F32)


def _rms(x, g):
    return x * lax.rsqrt(jnp.mean(x * x, axis=-1, keepdims=True) + EPS) * g


def _row_chunks(block_shape):
    g, r, _ = block_shape
    if g > 1 or r <= ROW_CHUNK:
        return [slice(None)]
    assert r % ROW_CHUNK == 0
    return [slice(c * ROW_CHUNK, (c + 1) * ROW_CHUNK) for c in range(r // ROW_CHUNK)]


def _mod_kernel(c_ref, w_ref, b_ref, o_ref):
    o_ref[...] = _dot(jax.nn.silu(c_ref[...]), w_ref[...]) + b_ref[...]


def _modulation(c_all, w_cond, b_cond, tn=2048):
    m, d = c_all.shape
    n = w_cond.shape[1]
    return pl.pallas_call(
        _mod_kernel,
        grid=(n // tn,),
        in_specs=[pl.BlockSpec((m, d), lambda j: (0, 0)),
                  pl.BlockSpec((d, tn), lambda j: (0, j)),
                  pl.BlockSpec((1, tn), lambda j: (0, j))],
        out_specs=pl.BlockSpec((m, tn), lambda j: (0, j)),
        out_shape=jax.ShapeDtypeStruct((m, n), F32),
        compiler_params=_cparams(("arbitrary",)),
        name="modulation",
    )(c_all, w_cond, b_cond.reshape(1, n))


class _Rows:
    def __init__(self, n_seq, seq_len, tile, mod_row0):
        if seq_len >= tile:
            assert seq_len % tile == 0
            self.G, self.R = 1, tile
            self.tiles_per_seq = seq_len // tile
            self.n_tiles = n_seq * self.tiles_per_seq
            self.mod_map = lambda i, k: (mod_row0 + i // self.tiles_per_seq, 0, k)
        else:
            assert seq_len % 8 == 0 and n_seq * seq_len <= tile
            self.G, self.R = n_seq, seq_len
            self.tiles_per_seq = 1
            self.n_tiles = 1
            mod_blk = mod_row0 // n_seq
            assert mod_blk * n_seq == mod_row0
            self.mod_map = lambda i, k: (mod_blk, 0, k)
        self.tm = self.G * self.R
        self.n_rows = n_seq * seq_len

    def x_spec(self, d, n_grid=2, **kw):
        if n_grid == 2:
            return pl.BlockSpec((self.G, self.R, d), lambda i, j: (i, 0, 0), **kw)
        return pl.BlockSpec((self.G, self.R, d), lambda i: (i, 0, 0), **kw)

    def mod_spec(self, d, k, n_grid=2):
        if n_grid == 2:
            return pl.BlockSpec((self.G, 1, d), lambda i, j: self.mod_map(i, k))
        return pl.BlockSpec((self.G, 1, d), lambda i: self.mod_map(i, k))

    def view(self, x):
        return x.reshape(self.n_tiles * self.G, self.R, x.shape[-1])


def _inproj_kernel(x_ref, sh_ref, sc_ref, g_ref, w_ref, *rest, starts, seg_out, n_out, emit):
    o_refs, h_ref = rest[:n_out], rest[-1]
    j = pl.program_id(1)
    w = w_ref[...].astype(BF16)
    if emit:
        rest[n_out][...] = w

    @pl.when(j == 0)
    def _():
        for rows in _row_chunks(x_ref.shape):
            h = _rms(x_ref[:, rows, :], g_ref[...]) * (1.0 + sc_ref[...]) + sh_ref[...]
            h = h.reshape(-1, h.shape[-1]).astype(h_ref.dtype)
            flat = rows if x_ref.shape[0] == 1 else slice(None)
            h_ref[flat, :] = h
            o_refs[seg_out[0]][flat, :] = _dot(h, w)

    for s, o in enumerate(seg_out):
        @pl.when((j >= max(starts[s], 1)) & (j < starts[s + 1]))
        def _(o_ref=o_refs[o]):
            o_ref[...] = _dot(h_ref[...], w)


def _inproj(rows, x3, mod3, g_pre, w_in, segments, out_widths, tn, emit):
    d = x3.shape[-1]
    assert sum(s[0] for s in segments) == w_in.shape[1]
    assert all(s[0] % tn == 0 and s[2] % tn == 0 for s in segments)
    starts = [0]
    for width, _, _ in segments:
        starts.append(starts[-1] + width // tn)
    n_steps = starts[-1]

    def out_spec(o):
        active = {}
        for s, (width, out, off) in enumerate(segments):
            if out == o:
                for t in range(width // tn):
                    active[starts[s] + t] = off // tn + t
        held, table = active[min(active)], []
        for step in range(n_steps):
            held = active.get(step, held)
            table.append(held)

        def col(j):
            c = jnp.int32(table[-1])
            for step in reversed(range(n_steps - 1)):
                c = jnp.where(j <= step, table[step], c)
            return c

        return pl.BlockSpec((rows.tm, tn), lambda i, j: (i, col(j)))

    w_spec = pl.BlockSpec((d, tn), lambda i, j: (0, j))
    kern = functools.partial(_inproj_kernel, starts=tuple(starts), seg_out=tuple(s[1] for s in segments),
                             n_out=len(out_widths), emit=emit)
    return pl.pallas_call(
        kern,
        grid=(rows.n_tiles, n_steps),
        in_specs=[rows.x_spec(d, pipeline_mode=pl.Buffered(1)),
                  rows.mod_spec(d, 0), rows.mod_spec(d, 1),
                  pl.BlockSpec((1, d), lambda i, j: (0, 0)),
                  w_spec],
        out_specs=[out_spec(o) for o in range(len(out_widths))] + ([w_spec] if emit else []),
        out_shape=[jax.ShapeDtypeStruct((rows.n_rows, w), F32) for w in out_widths]
                  + ([jax.ShapeDtypeStruct(w_in.shape, BF16)] if emit else []),
        scratch_shapes=[pltpu.VMEM((rows.tm, d), BF16)],
        compiler_params=_cparams(("parallel", "arbitrary")),
        name="inproj",
    )(x3, mod3, mod3, g_pre.reshape(1, d), w_in)


def _sb_logs(z, mask):
    t = jnp.log(1.0 + jnp.exp2(jnp.abs(z) * -LOG2E))
    log_beta = jnp.minimum(z, 0.0) - t
    neg_stay = z - log_beta
    if mask is not None:
        neg_stay = jnp.where(mask, neg_stay, 0.0)
    return log_beta, neg_stay


def _sb_weights(log_beta, neg_after, mask):
    w = jnp.exp2((log_beta - neg_after) * LOG2E)
    return w if mask is None else jnp.where(mask, w, 0.0)


def _tri(k):
    r = lax.broadcasted_iota(jnp.int32, (k, k), 0)
    c = lax.broadcasted_iota(jnp.int32, (k, k), 1)
    return (r > c).astype(BF16)


def _prompt_visit(bias_ref, tri, qs_ref, k_ref, v_ref, c_ref, acc_ref, start, blk, mask, first, n_heads):
    heads = range(n_heads)
    cols = [slice(h * HEAD_DIM, (h + 1) * HEAD_DIM) for h in heads]
    z = [_dot_nt(qs_ref[:, cols[h]], k_ref[pl.ds(start, blk), cols[h]]) + bias_ref[h] for h in heads]
    logs = [_sb_logs(z[h], mask) for h in heads]
    after = [_dot(logs[h][1].astype(BF16), tri) for h in heads]
    for h in heads:
        total = jnp.sum(logs[h][1], axis=-1, keepdims=True)
        if first:
            c_ref[h] = total
        else:
            after[h] = after[h] + c_ref[h]
            c_ref[h] += total
    w = [_sb_weights(logs[h][0], after[h], mask) for h in heads]
    for h in heads:
        pv = _dot(w[h], v_ref[pl.ds(start, blk), cols[h]])
        if first:
            acc_ref[h] = pv
        else:
            acc_ref[h] += pv


def _attn_prompt_kernel(bias_ref, tri_ref, q_ref, k_ref, v_ref, o_ref, qs_ref, c_ref, acc_ref, *, n_heads):
    blk = q_ref.shape[0]
    qi = pl.program_id(1)
    tri = tri_ref[...]
    qs_ref[...] = q_ref[...] * SB_SCALE

    def visit(start, mask, first):
        _prompt_visit(bias_ref, tri, qs_ref, k_ref, v_ref, c_ref, acc_ref, start, blk, mask, first, n_heads)

    r = lax.broadcasted_iota(jnp.int32, (blk, blk), 0)
    s = lax.broadcasted_iota(jnp.int32, (blk, blk), 1)
    visit(pl.multiple_of(qi * blk, blk), s < r, True)

    def body(it, _):
        visit(pl.multiple_of((qi - 1 - it) * blk, blk), None, False)
        return 0

    lax.fori_loop(0, qi, body, 0)
    for h in range(n_heads):
        o_ref[:, h * HEAD_DIM:(h + 1) * HEAD_DIM] = acc_ref[h]


def _attn_prompt(q, q_blk, k, v, sb_bias, n_seq, seq_len, n_heads, blk=ATT_BLOCK):
    nq = seq_len // blk
    att_w = n_heads * HEAD_DIM
    return pl.pallas_call(
        functools.partial(_attn_prompt_kernel, n_heads=n_heads),
        grid=(n_seq, nq),
        in_specs=[pl.BlockSpec(memory_space=pltpu.SMEM),
                  pl.BlockSpec((blk, blk), lambda b, i: (0, 0)),
                  pl.BlockSpec((blk, att_w), lambda b, i: (b * nq + i, q_blk)),
                  pl.BlockSpec((seq_len, att_w), lambda b, i: (b, 0)),
                  pl.BlockSpec((seq_len, att_w), lambda b, i: (b, 0))],
        out_specs=pl.BlockSpec((blk, att_w), lambda b, i: (b * nq + i, 0)),
        out_shape=jax.ShapeDtypeStruct((n_seq * seq_len, att_w), F32),
        scratch_shapes=[pltpu.VMEM((blk, att_w), F32),
                        pltpu.VMEM((n_heads, blk, 1), F32),
                        pltpu.VMEM((n_heads, blk, HEAD_DIM), F32)],
        compiler_params=_cparams(("parallel", "arbitrary")),
        name="attn_prompt",
    )(sb_bias, _tri(blk), q, k, v)


def _attn_sample_kernel(pt_ref, tri2_ref, bias_ref, q_ref, kn_ref, vn_ref, *rest, n_heads, t_new, pages):
    k_refs, v_refs = rest[:pages], rest[pages:2 * pages]
    o_ref, q2_ref, c_ref, acc_ref = rest[2 * pages:]
    s = pl.program_id(1)
    tri2 = tri2_ref[...]

    def head_masks(width):
        lane_head = lax.broadcasted_iota(jnp.int32, (t_new, width), 1) % n_heads
        return [lane_head == h for h in range(n_heads)]

    def lanes(x, j):
        return x[:, j * LANES:(j + 1) * LANES]

    def visit(kmats, vmats, bias, mask):
        hm = head_masks(kmats[0].shape[0])
        logs = []
        for km in kmats:
            zf = _dot_nt(q2_ref[...], km)
            z = zf[0:t_new]
            for h in range(1, n_heads):
                z = jnp.where(hm[h], zf[h * t_new:(h + 1) * t_new], z)
            logs.append(_sb_logs(z + bias, mask))
        n_ch = kmats[0].shape[0] // LANES
        order = [(i, j) for i in range(len(kmats)) for j in reversed(range(n_ch))]
        lhs = jnp.concatenate([lanes(logs[i][1], j) for i, j in order], axis=0)
        hi = lhs.astype(BF16)
        lo = (lhs - hi.astype(F32)).astype(BF16)
        st = _dot(hi, tri2) + _dot(lo, tri2)
        run = c_ref[...]
        after = {}
        for idx, ij in enumerate(order):
            blk = st[idx * t_new:(idx + 1) * t_new]
            after[ij] = blk[:, :LANES] + run
            run = run + blk[:, LANES:]
        c_ref[...] = run
        pv = None
        for i, vm in enumerate(vmats):
            aft = jnp.concatenate([after[(i, j)] for j in range(n_ch)], axis=1)
            w = _sb_weights(logs[i][0], aft, mask)
            we = jnp.concatenate([jnp.where(hm[h], w, 0.0) for h in range(n_heads)], axis=0)
            d = _dot(we, vm)
            pv = d if pv is None else pv + d
        acc_ref[...] += pv

    @pl.when(s == 0)
    def _():
        for h in range(n_heads):
            q2_ref[h * t_new:(h + 1) * t_new, :] = q_ref[:, h * HEAD_DIM:(h + 1) * HEAD_DIM] * SB_SCALE
        c_ref[...] = jnp.zeros_like(c_ref)
        acc_ref[...] = jnp.zeros_like(acc_ref)
        pad = jnp.zeros((LANES - kn_ref.shape[0], HEAD_DIM), F32)
        k_pos = lax.broadcasted_iota(jnp.int32, (t_new, LANES), 1) // n_heads
        q_pos = lax.broadcasted_iota(jnp.int32, (t_new, LANES), 0)
        visit([jnp.concatenate([kn_ref[...], pad], axis=0)], [jnp.concatenate([vn_ref[...], pad], axis=0)],
              bias_ref[:, :LANES], k_pos < q_pos)

    visit([r[...] for r in k_refs], [r[...] for r in v_refs], bias_ref[...], None)

    @pl.when(s == pl.num_programs(1) - 1)
    def _():
        for h in range(n_heads):
            o_ref[:, h * HEAD_DIM:(h + 1) * HEAD_DIM] = acc_ref[h * t_new:(h + 1) * t_new, :]


def _attn_sample(q, q_blk, k, v, cache_k, cache_v, page_table, sb_bias, n_seq, t_new, n_heads,
                 pages=PAGES_PER_STEP):
    n_phys, page = cache_k.shape[0], cache_k.shape[1]
    n_pages = page_table.shape[1]
    assert n_pages % pages == 0
    att_w = n_heads * HEAD_DIM
    rows = page * n_heads
    new_rows = t_new * n_heads
    assert LANES % n_heads == 0 and rows % LANES == 0 and new_rows <= LANES and t_new % 8 == 0
    ck = cache_k.reshape(n_phys, rows, HEAD_DIM)
    cv = cache_v.reshape(n_phys, rows, HEAD_DIM)
    kn = k.reshape(n_seq, new_rows, HEAD_DIM)
    vn = v.reshape(n_seq, new_rows, HEAD_DIM)
    bias = jnp.broadcast_to(jnp.tile(sb_bias, rows // n_heads)[None, :], (t_new, rows))
    r = lax.broadcasted_iota(jnp.int32, (LANES, LANES), 0)
    c = lax.broadcasted_iota(jnp.int32, (LANES, LANES), 1)
    same_head = (r % n_heads) == (c % n_heads)
    tri2 = jnp.concatenate([same_head & (r // n_heads > c // n_heads), same_head], axis=1).astype(BF16)

    def page_spec(i):
        return pl.BlockSpec((None, rows, HEAD_DIM),
                            lambda b, s, pt: (pt[b, n_pages - 1 - (s * pages + i)], 0, 0))

    grid_spec = pltpu.PrefetchScalarGridSpec(
        num_scalar_prefetch=1,
        grid=(n_seq, n_pages // pages),
        in_specs=[pl.BlockSpec((LANES, 2 * LANES), lambda b, s, pt: (0, 0)),
                  pl.BlockSpec((t_new, rows), lambda b, s, pt: (0, 0)),
                  pl.BlockSpec((t_new, att_w), lambda b, s, pt: (b, q_blk)),
                  pl.BlockSpec((None, new_rows, HEAD_DIM), lambda b, s, pt: (b, 0, 0)),
                  pl.BlockSpec((None, new_rows, HEAD_DIM), lambda b, s, pt: (b, 0, 0))]
                 + [page_spec(i) for i in range(pages)] * 2,
        out_specs=pl.BlockSpec((t_new, att_w), lambda b, s, pt: (b, 0)),
        scratch_shapes=[pltpu.VMEM((n_heads * t_new, HEAD_DIM), F32),
                        pltpu.VMEM((t_new, LANES), F32),
                        pltpu.VMEM((n_heads * t_new, HEAD_DIM), F32)],
    )
    kern = functools.partial(_attn_sample_kernel, n_heads=n_heads, t_new=t_new, pages=pages)
    return pl.pallas_call(
        kern,
        grid_spec=grid_spec,
        out_shape=jax.ShapeDtypeStruct((n_seq * t_new, att_w), F32),
        compiler_params=_cparams(("parallel", "arbitrary")),
        name="attn_sample",
    )(page_table, tri2, bias, q, kn, vn, *([ck] * pages), *([cv] * pages))


def _attn_fused_kernel(pt_ref, *refs, n_heads, t_new, pages, nq):
    n = 5 + 2 * pages
    sample_in, (biasp_ref, tri_ref, qp_ref, kp_ref, vp_ref) = refs[:n], refs[n:n + 5]
    o_ref, op_ref, q2_ref, c_ref, acc_ref, qs_ref, cp_ref, accp_ref = refs[n + 5:]
    _attn_sample_kernel(pt_ref, *sample_in, o_ref, q2_ref, c_ref, acc_ref, n_heads=n_heads, t_new=t_new, pages=pages)

    blk = qp_ref.shape[0]
    s = pl.program_id(1)
    qi = pl.program_id(0) % nq
    tri = tri_ref[...]

    def visit(mask, first):
        _prompt_visit(biasp_ref, tri, qs_ref, kp_ref, vp_ref, cp_ref, accp_ref, 0, blk, mask, first, n_heads)

    @pl.when(s == 0)
    def _():
        qs_ref[...] = qp_ref[...] * SB_SCALE
        r = lax.broadcasted_iota(jnp.int32, (blk, blk), 0)
        c = lax.broadcasted_iota(jnp.int32, (blk, blk), 1)
        visit(c < r, True)

    @pl.when((s > 0) & (s <= qi))
    def _():
        visit(None, False)

    @pl.when(s == qi)
    def _():
        for h in range(n_heads):
            op_ref[:, h * HEAD_DIM:(h + 1) * HEAD_DIM] = accp_ref[h]


def _attn_fused(glu_s, ks, vs, cache_k, cache_v, page_table, glu_p, kp, vp, q_blk, sb_bias, n_seq_s, t_new,
                n_seq_p, seq_len, n_heads, pages=PAGES_PER_STEP, blk=ATT_BLOCK):
    n_phys, page = cache_k.shape[0], cache_k.shape[1]
    n_pages = page_table.shape[1]
    nq = seq_len // blk
    assert n_pages % pages == 0 and n_seq_s == n_seq_p * nq and n_pages // pages >= nq
    att_w = n_heads * HEAD_DIM
    rows = page * n_heads
    new_rows = t_new * n_heads
    assert LANES % n_heads == 0 and rows % LANES == 0 and new_rows <= LANES and t_new % 8 == 0
    ck = cache_k.reshape(n_phys, rows, HEAD_DIM)
    cv = cache_v.reshape(n_phys, rows, HEAD_DIM)
    kn = ks.reshape(n_seq_s, new_rows, HEAD_DIM)
    vn = vs.reshape(n_seq_s, new_rows, HEAD_DIM)
    bias = jnp.broadcast_to(jnp.tile(sb_bias, rows // n_heads)[None, :], (t_new, rows))
    r = lax.broadcasted_iota(jnp.int32, (LANES, LANES), 0)
    c = lax.broadcasted_iota(jnp.int32, (LANES, LANES), 1)
    same_head = (r % n_heads) == (c % n_heads)
    tri2 = jnp.concatenate([same_head & (r // n_heads > c // n_heads), same_head], axis=1).astype(BF16)

    def page_spec(i):
        return pl.BlockSpec((None, rows, HEAD_DIM),
                            lambda b, s, pt: (pt[b, n_pages - 1 - (s * pages + i)], 0, 0))

    key_block = lambda b, s, pt: ((b // nq) * nq + jnp.maximum(b % nq - s, 0), 0)
    grid_spec = pltpu.PrefetchScalarGridSpec(
        num_scalar_prefetch=1,
        grid=(n_seq_s, n_pages // pages),
        in_specs=[pl.BlockSpec((LANES, 2 * LANES), lambda b, s, pt: (0, 0)),
                  pl.BlockSpec((t_new, rows), lambda b, s, pt: (0, 0)),
                  pl.BlockSpec((t_new, att_w), lambda b, s, pt: (b, q_blk)),
                  pl.BlockSpec((None, new_rows, HEAD_DIM), lambda b, s, pt: (b, 0, 0)),
                  pl.BlockSpec((None, new_rows, HEAD_DIM), lambda b, s, pt: (b, 0, 0))]
                 + [page_spec(i) for i in range(pages)] * 2
                 + [pl.BlockSpec(memory_space=pltpu.SMEM),
                    pl.BlockSpec((blk, blk), lambda b, s, pt: (0, 0)),
                    pl.BlockSpec((blk, att_w), lambda b, s, pt: (b, q_blk)),
                    pl.BlockSpec((blk, att_w), key_block),
                    pl.BlockSpec((blk, att_w), key_block)],
        out_specs=[pl.BlockSpec((t_new, att_w), lambda b, s, pt: (b, 0)),
                   pl.BlockSpec((blk, att_w), lambda b, s, pt: (b, 0))],
        scratch_shapes=[pltpu.VMEM((n_heads * t_new, HEAD_DIM), F32),
                        pltpu.VMEM((t_new, LANES), F32),
                        pltpu.VMEM((n_heads * t_new, HEAD_DIM), F32),
                        pltpu.VMEM((blk, att_w), F32),
                        pltpu.VMEM((n_heads, blk, 1), F32),
                        pltpu.VMEM((n_heads, blk, HEAD_DIM), F32)],
    )
    kern = functools.partial(_attn_fused_kernel, n_heads=n_heads, t_new=t_new, pages=pages, nq=nq)
    return pl.pallas_call(
        kern,
        grid_spec=grid_spec,
        out_shape=[jax.ShapeDtypeStruct((n_seq_s * t_new, att_w), F32),
                   jax.ShapeDtypeStruct((n_seq_p * seq_len, att_w), F32)],
        compiler_params=_cparams(("arbitrary", "arbitrary")),
        name="attn_fused",
    )(page_table, tri2, bias, glu_s, kn, vn, *([ck] * pages), *([cv] * pages), sb_bias, _tri(blk), glu_p, kp, vp)


def _pool_mix(u_ref, hist_ref, wg_ref, ps_ref, p_ref, ext_ref, pa_ref, pb_ref, *, G, R, t0, past_len, fresh):
    c_all = u_ref.shape[-1]
    cg = c_all // len(POOL_WINDOWS)
    n = POOL_HIST + R
    cur = u_ref[...].reshape(G, R, c_all)
    hist = hist_ref[...]
    if fresh:
        hist = jnp.where(t0 == 0, 0.0, hist)
    ext_ref[:, 0:POOL_HIST, :] = hist
    ext_ref[:, POOL_HIST:n, :] = cur

    src, dst = ext_ref, pa_ref
    for lvl, win in enumerate(POOL_WINDOWS):
        half = win // 2
        lo = lvl * cg
        rows_out = n - (win - 1)
        dst[:, 0:rows_out, lo:] = src[:, half:half + rows_out, lo:] + src[:, 0:rows_out, lo:]
        src, dst = dst, (pb_ref if dst is pa_ref else pa_ref)
        yield

    pos = past_len + t0 + lax.broadcasted_iota(jnp.int32, (1, R, cg), 1)
    for g, win in enumerate(POOL_WINDOWS):
        buf = pa_ref if g % 2 == 0 else pb_ref
        cols = slice(g * cg, (g + 1) * cg)
        first = POOL_HIST - (win - 1)
        wsum = buf[:, first:first + R, cols]
        cnt = jnp.minimum(win, pos + 1).astype(F32)
        pooled = wsum / cnt - cur[:, :, cols]
        mixed = _dot(pooled.reshape(G * R, cg), wg_ref[g])
        p_ref[:, cols] = (mixed * ps_ref[:, cols]).astype(p_ref.dtype)
        yield


def _mixer_kernel(a_ref, u_ref, hist_ref, gl_ref, x_ref, gt_ref, wg_ref, ps_ref, bg_ref, wa_ref, wp_ref, wo_ref,
                  gpost_ref, o_ref, p_ref, ext_ref, pa_ref, pb_ref, *, G, R, tiles_per_seq, past_len, fresh):
    d = x_ref.shape[-1]
    t0 = (pl.program_id(0) % tiles_per_seq) * R if fresh else 0
    stages = _pool_mix(u_ref, hist_ref, wg_ref, ps_ref, p_ref, ext_ref, pa_ref, pb_ref,
                       G=G, R=R, t0=t0, past_len=past_len, fresh=fresh)
    a = a_ref[...].astype(BF16)
    n_chunks = 2 * len(POOL_WINDOWS)
    cw = d // n_chunks
    parts = []
    for c in range(n_chunks):
        cols = slice(c * cw, (c + 1) * cw)
        parts.append(jax.nn.sigmoid(gl_ref[:, cols] + bg_ref[:, cols]) * _dot(a, wa_ref[:, cols]))
        next(stages, None)
    for _ in stages:
        pass
    merged_a = jnp.concatenate(parts, axis=1)
    merged = merged_a + jax.nn.sigmoid(gl_ref[:, d:] + bg_ref[:, d:]) * _dot(p_ref[...], wp_ref[...])
    y = _dot(merged.astype(BF16), wo_ref[...])
    o_ref[...] = x_ref[...] + gt_ref[...] * _rms(y, gpost_ref[...]).reshape(x_ref.shape)


def _mixer(rows, a, glu, u_blk, hist3, hist_map, x3, mod3, w_pool_group, pool_scale, b_gate, w_a, w_p, w_o, g_post,
           past_len, fresh):
    d = x3.shape[-1]
    n_groups, cg = w_pool_group.shape[0], w_pool_group.shape[1]
    pool_w, att_w = pool_scale.shape[0], a.shape[1]
    G, R, tm = rows.G, rows.R, rows.tm
    const = lambda shape: pl.BlockSpec(shape, lambda i: (0,) * len(shape), pipeline_mode=pl.Buffered(1))
    ext = pltpu.VMEM((G, POOL_HIST + R, pool_w), F32)
    kern = functools.partial(_mixer_kernel, G=G, R=R, tiles_per_seq=rows.tiles_per_seq,
                             past_len=past_len, fresh=fresh)
    return pl.pallas_call(
        kern,
        grid=(rows.n_tiles,),
        in_specs=[pl.BlockSpec((tm, att_w), lambda i: (i, 0)),
                  pl.BlockSpec((tm, pool_w), lambda i: (i, u_blk)),
                  pl.BlockSpec((G, POOL_HIST, pool_w), hist_map),
                  pl.BlockSpec((tm, 2 * d), lambda i: (i, 0)),
                  rows.x_spec(d, n_grid=1),
                  rows.mod_spec(d, 2, n_grid=1),
                  const((n_groups, cg, cg)), const((1, pool_w)), const((1, 2 * d)),
                  const((att_w, d)), const((pool_w, d)), const((d, d)), const((1, d))],
        out_specs=rows.x_spec(d, n_grid=1),
        out_shape=jax.ShapeDtypeStruct(x3.shape, F32),
        scratch_shapes=[pltpu.VMEM((tm, pool_w), BF16), ext, ext, ext],
        compiler_params=_cparams(("parallel",)),
        name="mixer",
    )(a, glu, hist3, glu, x3, mod3, w_pool_group, pool_scale.reshape(1, pool_w), b_gate.reshape(1, 2 * d),
      w_a, w_p, w_o, g_post.reshape(1, d))


def _ffn_kernel(x_ref, sh_ref, sc_ref, gt_ref, gpre_ref, gpost_ref, wg_ref, wu_ref, wo_ref, o_ref, *rest, emit):
    h_ref = rest[-1]
    j = pl.program_id(1)
    last = pl.num_programs(1) - 1
    wg, wu, wo = (r[...].astype(BF16) for r in (wg_ref, wu_ref, wo_ref))
    if emit:
        for ref, val in zip(rest[:3], (wg, wu, wo)):
            ref[...] = val

    def hidden(h):
        act = jax.nn.silu(_dot(h, wg)) * _dot(h, wu)
        return _dot(act.astype(BF16), wo)

    chunks = _row_chunks(x_ref.shape)
    flat = lambda rows: rows if x_ref.shape[0] == 1 else slice(None)

    @pl.when(j == 0)
    def _():
        for rows in chunks:
            h = _rms(x_ref[:, rows, :], gpre_ref[...]) * (1.0 + sc_ref[...]) + sh_ref[...]
            h = h.reshape(-1, h.shape[-1]).astype(h_ref.dtype)
            h_ref[flat(rows), :] = h
            o_ref[:, rows, :] = hidden(h).reshape(o_ref[:, rows, :].shape)

    @pl.when((j > 0) & (j < last))
    def _():
        o_ref[...] += hidden(h_ref[...]).reshape(o_ref.shape)

    @pl.when(j == last)
    def _():
        for rows in chunks:
            o = o_ref[:, rows, :]
            o = o + hidden(h_ref[flat(rows), :]).reshape(o.shape)
            o_ref[:, rows, :] = x_ref[:, rows, :] + gt_ref[...] * _rms(o, gpost_ref[...])


def _ffn(rows, x3, mod3, g_pre, g_post, wg, wu, wo, wu_col0, tf, emit):
    d = x3.shape[-1]
    d_ff = wo.shape[0]
    nf = d_ff // tf
    assert nf * tf == d_ff and nf >= 2 and wu_col0 % tf == 0
    u0 = wu_col0 // tf
    w_specs = [pl.BlockSpec((d, tf), lambda i, j: (0, j)),
               pl.BlockSpec((d, tf), lambda i, j: (0, u0 + j)),
               pl.BlockSpec((tf, d), lambda i, j: (j, 0))]
    emit_specs = [pl.BlockSpec((d, tf), lambda i, j: (0, j)),
                  pl.BlockSpec((d, tf), lambda i, j: (0, j)),
                  pl.BlockSpec((tf, d), lambda i, j: (j, 0))]
    emit_shapes = [jax.ShapeDtypeStruct((d, d_ff), BF16), jax.ShapeDtypeStruct((d, d_ff), BF16),
                   jax.ShapeDtypeStruct((d_ff, d), BF16)]
    return pl.pallas_call(
        functools.partial(_ffn_kernel, emit=emit),
        grid=(rows.n_tiles, nf),
        in_specs=[rows.x_spec(d, pipeline_mode=pl.Buffered(1)),
                  rows.mod_spec(d, 3), rows.mod_spec(d, 4), rows.mod_spec(d, 5),
                  pl.BlockSpec((1, d), lambda i, j: (0, 0)),
                  pl.BlockSpec((1, d), lambda i, j: (0, 0))] + w_specs,
        out_specs=[rows.x_spec(d)] + (emit_specs if emit else []),
        out_shape=[jax.ShapeDtypeStruct(x3.shape, F32)] + (emit_shapes if emit else []),
        scratch_shapes=[pltpu.VMEM((rows.tm, d), BF16)],
        compiler_params=_cparams(("parallel", "arbitrary")),
        name="ffn",
    )(x3, mod3, mod3, mod3, g_pre.reshape(1, d), g_post.reshape(1, d), wg, wu, wo)


def kernel(x_prompt, x_sample, c_prompt, c_sample, cache_k, cache_v, page_table, state_pool, w_cond, b_cond, g_mix_pre, g_mix_post, g_ffn_pre, g_ffn_post, w_in, b_gate, sb_bias, w_pool_group, pool_scale, w_branch_att, w_branch_pool, w_out, w_ffn_in, w_ffn_out):
    depth = w_cond.shape[0]
    b_p, t_p, d = x_prompt.shape
    b_s, t_s, _ = x_sample.shape
    n_heads = sb_bias.shape[1]
    att_w = n_heads * HEAD_DIM
    pool_w = pool_scale.shape[1]
    d_ff = w_ffn_out.shape[1]
    past_len = page_table.shape[1] * cache_k.shape[2]
    gl_w = w_in.shape[2] - 3 * att_w - pool_w
    assert gl_w % att_w == 0 and (gl_w + att_w) % pool_w == 0
    segments = ((att_w, 0, gl_w), (att_w, 1, 0), (att_w, 2, 0), (pool_w, 0, gl_w + att_w), (gl_w, 0, 0))
    out_widths = (gl_w + att_w + pool_w, att_w, att_w)
    q_blk, u_blk, u_col = gl_w // att_w, (gl_w + att_w) // pool_w, gl_w + att_w
    rows_p = lambda tile: _Rows(b_p, t_p, tile, mod_row0=b_s)
    rows_s = lambda tile: _Rows(b_s, t_s, tile, mod_row0=0)
    pad = (-(b_s + b_p)) % 8
    c_all = jnp.concatenate([c_sample, c_prompt, jnp.zeros((pad, d), F32)], axis=0)

    yp, ys = x_prompt, x_sample
    outs = [[] for _ in range(6)]
    for l in range(depth):
        mod = _modulation(c_all, w_cond[l], b_cond[l])
        mod3 = mod.reshape(mod.shape[0], 1, mod.shape[1])

        rs, rp = rows_s(TILE_INPROJ), rows_p(TILE_INPROJ)
        glu_s, ks, vs, w_in_b = _inproj(rs, rs.view(ys), mod3, g_mix_pre[l], w_in[l], segments, out_widths,
                                        tn=COLS_SHORT[0], emit=True)
        glu_p, kp, vp = _inproj(rp, rp.view(yp), mod3, g_mix_pre[l], w_in_b, segments, out_widths,
                                tn=COLS_LONG[0], emit=False)

        a_s, a_p = _attn_fused(glu_s, ks, vs, cache_k[l], cache_v[l], page_table, glu_p, kp, vp, q_blk,
                               sb_bias[l], b_s, t_s, b_p, t_p, n_heads)

        w_a, w_p, w_o = (w.astype(BF16) for w in (w_branch_att[l], w_branch_pool[l], w_out[l]))
        rs, rp = rows_s(TILE_MIXER), rows_p(TILE_MIXER)
        hist_s = jnp.pad(state_pool[l], ((0, 0), (POOL_HIST - POOL_BUF, 0), (0, 0)))
        ys = _mixer(rs, a_s, glu_s, u_blk, hist_s, lambda i: (0, 0, 0), rs.view(ys), mod3, w_pool_group[l],
                    pool_scale[l], b_gate[l], w_a, w_p, w_o, g_mix_post[l], past_len=past_len, fresh=False)
        per_tile = rp.R // POOL_HIST
        yp = _mixer(rp, a_p, glu_p, u_blk, glu_p.reshape(glu_p.shape[0] // POOL_HIST, POOL_HIST, glu_p.shape[1]),
                    lambda i: (jnp.maximum(i * per_tile - 1, 0), 0, u_blk), rp.view(yp), mod3, w_pool_group[l],
                    pool_scale[l], b_gate[l], w_a, w_p, w_o, g_mix_post[l], past_len=0, fresh=True)

        rs, rp = rows_s(TILE_FFN), rows_p(TILE_FFN)
        ys, wg_b, wu_b, wo_b = _ffn(rs, rs.view(ys), mod3, g_ffn_pre[l], g_ffn_post[l], w_ffn_in[l], w_ffn_in[l],
                                    w_ffn_out[l], wu_col0=d_ff, tf=COLS_SHORT[1], emit=True)
        yp, = _ffn(rp, rp.view(yp), mod3, g_ffn_pre[l], g_ffn_post[l], wg_b, wu_b, wo_b,
                   wu_col0=0, tf=COLS_LONG[1], emit=False)
        yp, ys = yp.reshape(b_p, t_p, d), ys.reshape(b_s, t_s, d)

        up3 = glu_p.reshape(b_p, t_p, -1)[:, t_p - POOL_BUF:, u_col:u_col + pool_w]
        us3 = glu_s.reshape(b_s, t_s, -1)[:, :, u_col:u_col + pool_w]
        new = (kp.reshape(b_p, t_p, n_heads, HEAD_DIM), vp.reshape(b_p, t_p, n_heads, HEAD_DIM),
               up3,
               ks.reshape(b_s, t_s, n_heads, HEAD_DIM), vs.reshape(b_s, t_s, n_heads, HEAD_DIM),
               jnp.concatenate([state_pool[l], us3], axis=1)[:, -POOL_BUF:, :])
        for lst, val in zip(outs, new):
            lst.append(val)
    return (yp, ys) + tuple(jnp.stack(o) for o in outs)
```
